```python
import math
import jax, jax.numpy as jnp
from jax import lax
import numpy as np

D_MODEL = 2048
BATCH = 16
SEQ = 2048
DEPTH = 2

EPS = 1e-6
NEG_INF = -1e30
ROPE_BASE = 10000.0

RET_HEADS = 8
RET_DIM = 128
RET_WIDTH = RET_HEADS * RET_DIM
RET_CHUNK = 128
LRU_WIDTH = D_MODEL // 2
LRU_BLOCKS = 8
LRU_BLOCK_DIM = LRU_WIDTH // LRU_BLOCKS
LRU_CONV = 4
LRU_C = 8.0
E_SPLITS = (RET_WIDTH, 2 * RET_WIDTH, 3 * RET_WIDTH, 4 * RET_WIDTH, 4 * RET_WIDTH + LRU_WIDTH)
E_IN = 4 * RET_WIDTH + 2 * LRU_WIDTH
E_MIX = RET_WIDTH + LRU_WIDTH

ATT_HEADS = 16
ATT_DIM = D_MODEL // ATT_HEADS
KV_HEADS = 4
IDX_HEADS = 16
IDX_DIM = 64
TOPK_MAX = 256
Q_BLOCK = 128
O_Q = ATT_HEADS * ATT_DIM
O_KV = KV_HEADS * ATT_DIM
O_QI = IDX_HEADS * IDX_DIM
O_SPLITS = (O_Q, O_Q + O_KV, O_Q + 2 * O_KV, O_Q + 2 * O_KV + O_QI, O_Q + 2 * O_KV + O_QI + IDX_DIM)
O_IN = O_SPLITS[-1] + IDX_HEADS

REL_BUCKETS = 32
REL_MAX_DIST = 128

D_FF = 5632
FFN_CONV = 3

N_EVEN = (DEPTH + 1) // 2
N_ODD = DEPTH // 2

kernel_name = 'hybrid_retention_rglru_dsa_convffn'


def rmsnorm(x, g):
    xf = x.astype(jnp.float32)
    y = xf * lax.rsqrt(jnp.mean(xf * xf, axis=-1, keepdims=True) + EPS)
    return (y * g.astype(jnp.float32)).astype(x.dtype)


def causal_dwconv(x, w, b):
    width = w.shape[0]
    S = x.shape[1]
    xp = jnp.pad(x, ((0, 0), (width - 1, 0), (0, 0)))
    out = b
    for j in range(width):
        out = out + w[j] * xp[:, j:j + S]
    return out


def rotary(x, pos):
    half = x.shape[-1] // 2
    freqs = ROPE_BASE ** (-jnp.arange(half, dtype=jnp.float32) / half)
    ang = pos.astype(jnp.float32)[:, None] * freqs[None, :]
    cos = jnp.cos(ang)[None, :, None, :]
    sin = jnp.sin(ang)[None, :, None, :]
    x1, x2 = x[..., :half], x[..., half:]
    return jnp.concatenate([x1 * cos - x2 * sin, x2 * cos + x1 * sin], axis=-1)


def retention_chunkwise(q, k, v):
    B, S, H, Dh = q.shape
    C = RET_CHUNK
    NC = S // C
    log_g = jnp.log1p(-jnp.exp2(-5.0 - jnp.arange(H, dtype=jnp.float32)))
    pos = jnp.arange(C, dtype=jnp.float32)
    diff = pos[:, None] - pos[None, :]
    inner_decay = jnp.where(diff[None] >= 0, jnp.exp(jnp.maximum(diff, 0.0)[None] * log_g[:, None, None]), 0.0)
    q_decay = jnp.exp((pos[None, :] + 1.0) * log_g[:, None])
    k_decay = jnp.exp((C - 1.0 - pos[None, :]) * log_g[:, None])
    chunk_decay = jnp.exp(C * log_g)
    qc = q.reshape(B, NC, C, H, Dh)
    kc = k.reshape(B, NC, C, H, Dh)
    vc = v.reshape(B, NC, C, H, Dh)
    scores = jnp.einsum('bnihd,bnjhd->bnhij', qc, kc) * inner_decay
    y_inner = jnp.einsum('bnhij,bnjhd->bnihd', scores, vc)
    kv = jnp.einsum('bnjhd,hj,bnjhe->bnhde', kc, k_decay, vc)

    def step(state, kv_n):
        return chunk_decay[None, :, None, None] * state + kv_n, state

    _, prev = lax.scan(step, jnp.zeros((B, H, Dh, Dh), jnp.float32), jnp.moveaxis(kv, 1, 0))
    prev = jnp.moveaxis(prev, 0, 1)
    y_cross = jnp.einsum('bnihd,hi,bnhde->bnihe', qc, q_decay, prev)
    return (y_inner + y_cross).reshape(B, S, H, Dh)


def rg_lru(x, wa, ba, wx, bx, lam):
    B, S, W = x.shape
    xf = x.astype(jnp.float32)
    xb = xf.reshape(B, S, LRU_BLOCKS, LRU_BLOCK_DIM)
    r = jax.nn.sigmoid(jnp.einsum('bsnd,nde->bsne', xb, wa).reshape(B, S, W) + ba)
    i = jax.nn.sigmoid(jnp.einsum('bsnd,nde->bsne', xb, wx).reshape(B, S, W) + bx)
    log_a = -LRU_C * r * jax.nn.softplus(-lam)
    a = jnp.exp(log_a)
    mult = jnp.sqrt(jnp.maximum(-jnp.expm1(2.0 * log_a), 0.0))
    b = mult * (i * xf)

    def combine(c1, c2):
        a1, b1 = c1
        a2, b2 = c2
        return a1 * a2, a2 * b1 + b2

    _, h = lax.associative_scan(combine, (a, b), axis=1)
    return h


def hybrid_even_mixer(h, w_in, ret_gn, conv_w, conv_b, wa, ba, wx, bx, lam, w_out):
    B, S, _ = h.shape
    proj = h @ w_in
    q, k, v, g, xr, yr = jnp.split(proj, E_SPLITS, axis=-1)
    pos = jnp.arange(S)
    q = rotary(q.astype(jnp.float32).reshape(B, S, RET_HEADS, RET_DIM), pos)
    k = rotary(k.astype(jnp.float32).reshape(B, S, RET_HEADS, RET_DIM), pos) * (RET_DIM ** -0.5)
    v = v.astype(jnp.float32).reshape(B, S, RET_HEADS, RET_DIM)
    y = retention_chunkwise(q, k, v)
    y = y * lax.rsqrt(jnp.mean(y * y, axis=-1, keepdims=True) + EPS)
    y = y.reshape(B, S, RET_WIDTH) * ret_gn.astype(jnp.float32)
    ret_out = y * jax.nn.silu(g.astype(jnp.float32))
    xc = causal_dwconv(xr, conv_w, conv_b)
    lru_out = rg_lru(xc, wa, ba, wx, bx, lam) * jax.nn.gelu(yr.astype(jnp.float32))
    mixed = jnp.concatenate([ret_out, lru_out], axis=-1).astype(h.dtype)
    return mixed @ w_out


def t5_bucket(rel):
    n = jnp.maximum(rel, 0)
    max_exact = REL_BUCKETS // 2
    nf = jnp.maximum(n, max_exact).astype(jnp.float32)
    large = max_exact + (jnp.log(nf / max_exact) / math.log(REL_MAX_DIST / max_exact)
                         * (REL_BUCKETS - max_exact)).astype(jnp.int32)
    large = jnp.minimum(large, REL_BUCKETS - 1)
    return jnp.where(n < max_exact, n, large)


def dsa_attention(h, w_in, w_out, rel_bias):
    B, S, _ = h.shape
    topk = min(TOPK_MAX, S // 4)
    nb = S // Q_BLOCK
    G = ATT_HEADS // KV_HEADS
    proj = h @ w_in
    q, k, v, qi, ki, wi = jnp.split(proj, O_SPLITS, axis=-1)
    q = q.reshape(B, S, KV_HEADS, G, ATT_DIM)
    k = k.reshape(B, S, KV_HEADS, ATT_DIM)
    v = v.reshape(B, S, KV_HEADS, ATT_DIM)
    qi = qi.reshape(B, S, IDX_HEADS, IDX_DIM)
    wi = wi * (IDX_HEADS ** -0.5 * IDX_DIM ** -0.5)
    kpos = jnp.arange(S)

    def to_blocks(t):
        return jnp.moveaxis(t.reshape((B, nb, Q_BLOCK) + t.shape[2:]), 1, 0)

    def attend_block(blk):
        qb, qib, wib, qpos = blk
        s = jax.nn.relu(jnp.einsum('bqhd,bsd->bqhs', qib, ki).astype(jnp.float32))
        score = jnp.einsum('bqhs,bqh->bqs', s, wib.astype(jnp.float32))
        score = jnp.where(kpos[None, None, :] <= qpos[None, :, None], score, NEG_INF)
        _, idx = lax.top_k(score, topk)
        kg = jax.vmap(lambda kk, ii: kk[ii])(k, idx)
        vg = jax.vmap(lambda vv, ii: vv[ii])(v, idx)
        logits = jnp.einsum('bqkgd,bqjkd->bqkgj', qb, kg).astype(jnp.float32) * (ATT_DIM ** -0.5)
        bias = rel_bias[t5_bucket(qpos[None, :, None] - idx)]
        bias = jnp.transpose(bias.reshape(B, Q_BLOCK, topk, KV_HEADS, G), (0, 1, 3, 4, 2))
        valid = (idx <= qpos[None, :, None])[:, :, None, None, :]
        logits = jnp.where(valid, logits + bias.astype(jnp.float32), NEG_INF)
        p = jax.nn.softmax(logits, axis=-1)
        out = jnp.einsum('bqkgj,bqjkd->bqkgd', p.astype(vg.dtype), vg)
        return out.reshape(B, Q_BLOCK, O_Q)

    qpos_blocks = jnp.arange(S).reshape(nb, Q_BLOCK)
    out = lax.map(attend_block, (to_blocks(q), to_blocks(qi), to_blocks(wi), qpos_blocks))
    out = jnp.moveaxis(out, 0, 1).reshape(B, S, O_Q)
    return out @ w_out


def conv_ffn(h, w_gu, conv_w, conv_b, w_down):
    gu = h @ w_gu
    g, u = jnp.split(gu, [D_FF], axis=-1)
    g = causal_dwconv(g, conv_w, conv_b)
    return (jax.nn.silu(g) * u) @ w_down


def setup_inputs(seed: int = 0) -> dict:
    key = jax.random.key(seed)
    ks = jax.random.split(key, 21)
    f32 = jnp.float32

    def nrm(k, shape, fan_in):
        return jax.random.normal(k, shape, f32) * (fan_in ** -0.5)

    def gain(k, shape):
        return 1.0 + 0.02 * jax.random.normal(k, shape, f32)

    def small(k, shape):
        return 0.02 * jax.random.normal(k, shape, f32)

    u = jax.random.uniform(ks[11], (N_EVEN, LRU_WIDTH), f32, 0.9, 0.999)
    s = u ** (1.0 / LRU_C)
    e_lambda = jnp.log(s) - jnp.log1p(-s)
    return {
        'x': jax.random.normal(ks[0], (BATCH, SEQ, D_MODEL), f32),
        'norm_mix': gain(ks[1], (DEPTH, D_MODEL)),
        'norm_ffn': gain(ks[2], (DEPTH, D_MODEL)),
        'e_w_in': nrm(ks[3], (N_EVEN, D_MODEL, E_IN), D_MODEL),
        'e_ret_gn': gain(ks[4], (N_EVEN, RET_WIDTH)),
        'e_conv_w': nrm(ks[5], (N_EVEN, LRU_CONV, LRU_WIDTH), LRU_CONV),
        'e_conv_b': small(ks[6], (N_EVEN, LRU_WIDTH)),
        'e_gate_a_w': nrm(ks[7], (N_EVEN, LRU_BLOCKS, LRU_BLOCK_DIM, LRU_BLOCK_DIM), LRU_BLOCK_DIM),
        'e_gate_a_b': small(ks[8], (N_EVEN, LRU_WIDTH)),
        'e_gate_x_w': nrm(ks[9], (N_EVEN, LRU_BLOCKS, LRU_BLOCK_DIM, LRU_BLOCK_DIM), LRU_BLOCK_DIM),
        'e_gate_x_b': small(ks[10], (N_EVEN, LRU_WIDTH)),
        'e_lambda': e_lambda,
        'e_w_out': nrm(ks[12], (N_EVEN, E_MIX, D_MODEL), E_MIX),
        'o_w_in': nrm(ks[13], (N_ODD, D_MODEL, O_IN), D_MODEL),
        'o_w_out': nrm(ks[14], (N_ODD, O_Q, D_MODEL), O_Q),
        'rel_bias': 0.5 * jax.random.normal(ks[15], (REL_BUCKETS, ATT_HEADS), f32),
        'ffn_w_gu': nrm(ks[16], (DEPTH, D_MODEL, 2 * D_FF), D_MODEL),
        'ffn_conv_w': nrm(ks[17], (DEPTH, FFN_CONV, D_FF), FFN_CONV),
        'ffn_conv_b': small(ks[18], (DEPTH, D_FF)),
        'ffn_w_down': nrm(ks[19], (DEPTH, D_FF, D_MODEL), D_FF),
        'final_norm': gain(ks[20], (D_MODEL,)),
    }


def reference(x, norm_mix, norm_ffn, e_w_in, e_ret_gn, e_conv_w, e_conv_b, e_gate_a_w, e_gate_a_b,
              e_gate_x_w, e_gate_x_b, e_lambda, e_w_out, o_w_in, o_w_out, rel_bias,
              ffn_w_gu, ffn_conv_w, ffn_conv_b, ffn_w_down, final_norm):
    for layer in range(DEPTH):
        h = rmsnorm(x, norm_mix[layer])
        if layer % 2 == 0:
            e = layer // 2
            mix = hybrid_even_mixer(h, e_w_in[e], e_ret_gn[e], e_conv_w[e], e_conv_b[e],
                                    e_gate_a_w[e], e_gate_a_b[e], e_gate_x_w[e], e_gate_x_b[e],
                                    e_lambda[e], e_w_out[e])
        else:
            o = layer // 2
            mix = dsa_attention(h, o_w_in[o], o_w_out[o], rel_bias)
        x = x + mix.astype(x.dtype)
        h = rmsnorm(x, norm_ffn[layer])
        x = x + conv_ffn(h, ffn_w_gu[layer], ffn_conv_w[layer], ffn_conv_b[layer], ffn_w_down[layer]).astype(x.dtype)
    return rmsnorm(x, final_norm)
```

```python
import functools
import math

import numpy as np
import jax
import jax.numpy as jnp
from jax import lax
from jax.experimental import pallas as pl
from jax.experimental.pallas import tpu as pltpu

F32 = jnp.float32
BF16 = jnp.bfloat16
I32 = jnp.int32

EPS = 1e-6
NEG_INF = -1e30
ROPE_BASE = 10000.0

RET_HEADS = 8
RET_DIM = 128
RET_WIDTH = RET_HEADS * RET_DIM
RET_CHUNK = 128
LRU_BLOCKS = 8
LRU_BLOCK_DIM = 128
LRU_WIDTH = LRU_BLOCKS * LRU_BLOCK_DIM
LRU_CONV = 4
LRU_C = 8.0

ATT_HEADS = 16
ATT_DIM = 128
KV_HEADS = 4
GROUP = ATT_HEADS // KV_HEADS
IDX_HEADS = 16
IDX_DIM = 64
TOPK_MAX = 256
REL_BUCKETS = 32
REL_MAX_DIST = 128
FFN_CONV = 3

LANES = 128
SUBLANES = 8
MIB = 2 ** 20
INT_MIN = -2 ** 31


def _cparams(n_axes, vmem_mib):
    return pltpu.CompilerParams(
        dimension_semantics=("arbitrary",) * n_axes,
        vmem_limit_bytes=int(vmem_mib * MIB))


def _dot(a, b):
    return jnp.dot(a, b, preferred_element_type=F32)


def _dot_nt(a, b):
    return lax.dot_general(a, b, (((1,), (1,)), ((), ())), preferred_element_type=F32)


def _dot_tn(a, b):
    return lax.dot_general(a, b, (((0,), (0,)), ((), ())), preferred_element_type=F32)


def _rms_rows(x, g):
    ms = jnp.mean(x * x, axis=-1, keepdims=True)
    return x * lax.rsqrt(ms + EPS) * g


def _rms_to_ref(x_ref, g_ref, h_ref, row_chunk):
    g = g_ref[...]

    def body(r, c):
        sl = pl.ds(pl.multiple_of(r * row_chunk, row_chunk), row_chunk)
        h_ref[sl, :] = _rms_rows(x_ref[sl, :], g).astype(h_ref.dtype)
        return c

    lax.fori_loop(0, x_ref.shape[0] // row_chunk, body, 0)


def _norm_matmul_kernel(x_ref, g_ref, w_ref, o_ref, h_ref, *, row_chunk):
    @pl.when(pl.program_id(1) == 0)
    def _():
        _rms_to_ref(x_ref, g_ref, h_ref, row_chunk)

    o_ref[...] = _dot(h_ref[...], w_ref[...]).astype(o_ref.dtype)


def norm_matmul(x, g, w, *, tm, tn, out_dtype=F32, row_chunk=128):
    m, d = x.shape
    n = w.shape[1]
    vmem = (2 * tm * d * 4 + 2 * d * tn * 2 + 2 * tm * tn * 4 + tm * d * 2) / MIB + 8
    return pl.pallas_call(
        functools.partial(_norm_matmul_kernel, row_chunk=row_chunk),
        grid=(m // tm, n // tn),
        in_specs=[
            pl.BlockSpec((tm, d), lambda i, j: (i, 0)),
            pl.BlockSpec((1, d), lambda i, j: (0, 0)),
            pl.BlockSpec((d, tn), lambda i, j: (0, j)),
        ],
        out_specs=pl.BlockSpec((tm, tn), lambda i, j: (i, j)),
        out_shape=jax.ShapeDtypeStruct((m, n), out_dtype),
        scratch_shapes=[pltpu.VMEM((tm, d), BF16)],
        compiler_params=_cparams(2, vmem),
        name="norm_matmul",
    )(x, g.reshape(1, d), w)


def _matmul_residual_kernel(x_ref, a1_ref, a2_ref, w1_ref, w2_ref, o_ref):
    o_ref[...] = x_ref[...] + _dot(a1_ref[...], w1_ref[...]) + _dot(a2_ref[...], w2_ref[...])


def matmul_residual(x, a1, c1, a2, c2, w, *, tm):
    m, n = x.shape
    k1 = w.shape[0] // 2
    vmem = (4 * tm * n * 4 + 4 * tm * k1 * 2 + 4 * k1 * n * 2) / MIB + 8
    return pl.pallas_call(
        _matmul_residual_kernel,
        grid=(m // tm,),
        in_specs=[
            pl.BlockSpec((tm, n), lambda i: (i, 0)),
            pl.BlockSpec((tm, k1), lambda i: (i, c1)),
            pl.BlockSpec((tm, k1), lambda i: (i, c2)),
            pl.BlockSpec((k1, n), lambda i: (0, 0)),
            pl.BlockSpec((k1, n), lambda i: (1, 0)),
        ],
        out_specs=pl.BlockSpec((tm, n), lambda i: (i, 0)),
        out_shape=jax.ShapeDtypeStruct((m, n), F32),
        compiler_params=_cparams(1, vmem),
        name="matmul_residual",
    )(x, a1, a2, w, w)


def _ffn_kernel(x_ref, g_ref, wg_ref, wu_ref, cw_ref, cb_ref, wd_ref, gf_ref, o_ref,
                h_ref, carry_ref, *, tm, seq, row_chunk, final_norm):
    i = pl.program_id(0)
    j = pl.program_id(1)
    nj = pl.num_programs(1)

    @pl.when(j == 0)
    def _():
        _rms_to_ref(x_ref, g_ref, h_ref, row_chunk)

    h = h_ref[...]
    g = _dot(h, wg_ref[...])
    u = _dot(h, wu_ref[...])

    seq_start = (i * tm) % seq == 0
    prev = jnp.where(seq_start, 0.0, carry_ref[j])
    carry_ref[j] = g[tm - SUBLANES:, :]
    row = lax.broadcasted_iota(I32, (SUBLANES, g.shape[1]), 0)
    g1 = pltpu.roll(g, 1, 0)
    g2 = pltpu.roll(g, 2, 0)
    g1_top = jnp.where(row == 0, prev[7:8, :], g1[:SUBLANES])
    g2_top = jnp.where(row == 0, prev[6:7, :], jnp.where(row == 1, prev[7:8, :], g2[:SUBLANES]))
    g1 = jnp.concatenate([g1_top, g1[SUBLANES:]], axis=0)
    g2 = jnp.concatenate([g2_top, g2[SUBLANES:]], axis=0)
    cw = cw_ref[...]
    gc = cb_ref[...] + cw[0:1] * g2 + cw[1:2] * g1 + cw[2:3] * g
    act = (gc * jax.nn.sigmoid(gc) * u).astype(BF16)
    contrib = _dot(act, wd_ref[...])

    @pl.when(j == 0)
    def _():
        o_ref[...] = x_ref[...] + contrib

    @pl.when(j > 0)
    def _():
        o_ref[...] += contrib

    if final_norm:
        @pl.when(j == nj - 1)
        def _():
            _rms_to_ref(o_ref, gf_ref, o_ref, row_chunk)


def conv_ffn(x, g, w_gu, conv_w, conv_b, w_down, g_final, *, seq, tm, tf,
             final_norm, row_chunk=128):
    m, d = x.shape
    f = w_down.shape[0]
    nf = f // tf
    vmem = (4 * tm * d * 4 + tm * d * 2 + 4 * d * tf * 2 + 2 * tf * d * 2
            + 6 * tm * tf * 4) / MIB + 6
    return pl.pallas_call(
        functools.partial(_ffn_kernel, tm=tm, seq=seq, row_chunk=row_chunk,
                          final_norm=final_norm),
        grid=(m // tm, nf),
        in_specs=[
            pl.BlockSpec((tm, d), lambda i, j: (i, 0)),
            pl.BlockSpec((1, d), lambda i, j: (0, 0)),
            pl.BlockSpec((d, tf), lambda i, j: (0, j)),
            pl.BlockSpec((d, tf), lambda i, j: (0, j + nf)),
            pl.BlockSpec((FFN_CONV, tf), lambda i, j: (0, j)),
            pl.BlockSpec((1, tf), lambda i, j: (0, j)),
            pl.BlockSpec((tf, d), lambda i, j: (j, 0)),
            pl.BlockSpec((1, d), lambda i, j: (0, 0)),
        ],
        out_specs=pl.BlockSpec((tm, d), lambda i, j: (i, 0)),
        out_shape=jax.ShapeDtypeStruct((m, d), F32),
        scratch_shapes=[pltpu.VMEM((tm, d), BF16),
                        pltpu.VMEM((nf, SUBLANES, tf), F32)],
        compiler_params=_cparams(2, vmem),
        name="conv_ffn",
    )(x, g.reshape(1, d), w_gu, w_gu, conv_w, conv_b.reshape(1, f), w_down,
      g_final.reshape(1, d))


def _retention_kernel(q_ref, k_ref, v_ref, gate_ref, cos_ref, sin_ref, dec_ref, qd_ref,
                      kd_ref, cd_ref, gn_ref, o_ref, *, seq):
    c = RET_CHUNK
    inner = dec_ref[0]
    qd = qd_ref[0]
    kd = kd_ref[0]
    cdec = cd_ref[0][0:1, :]
    gn = gn_ref[...]
    scale = RET_DIM ** -0.5

    def chunk(n, state):
        sl = pl.ds(pl.multiple_of(n * c, c), c)
        cos = cos_ref[sl, :]
        sin = sin_ref[sl, :]
        q = q_ref[0, sl, :]
        k = k_ref[0, sl, :]
        v = v_ref[0, sl, :].astype(BF16)
        qr = q * cos + pltpu.roll(q, RET_DIM // 2, 1) * sin
        kr = (k * cos + pltpu.roll(k, RET_DIM // 2, 1) * sin) * scale
        scores = _dot_nt(qr.astype(BF16), kr.astype(BF16)) * inner
        y = _dot(scores.astype(BF16), v)
        y = y + _dot((qr * qd).astype(BF16), state.astype(BF16))
        kv = _dot_tn((kr * kd).astype(BF16), v)
        y = y * lax.rsqrt(jnp.mean(y * y, axis=-1, keepdims=True) + EPS)
        y = y * gn
        gate = gate_ref[0, sl, :]
        o_ref[0, sl, :] = (y * (gate * jax.nn.sigmoid(gate))).astype(o_ref.dtype)
        return cdec * state + kv

    lax.fori_loop(0, seq // c, chunk, jnp.zeros((RET_DIM, RET_DIM), F32))


def retention(proj, gn, *, seq):
    b = proj.shape[0]
    h_ = RET_HEADS
    c = RET_CHUNK
    half = RET_DIM // 2
    freqs = ROPE_BASE ** (-jnp.arange(half, dtype=F32) / half)
    ang = jnp.arange(seq, dtype=F32)[:, None] * freqs[None, :]
    cos2 = jnp.concatenate([jnp.cos(ang), jnp.cos(ang)], axis=-1)
    sin2 = jnp.concatenate([-jnp.sin(ang), jnp.sin(ang)], axis=-1)
    log_g = jnp.log1p(-jnp.exp2(-5.0 - jnp.arange(h_, dtype=F32)))
    pos = jnp.arange(c, dtype=F32)
    diff = pos[:, None] - pos[None, :]
    inner = jnp.where(diff[None] >= 0,
                      jnp.exp(jnp.maximum(diff, 0.0)[None] * log_g[:, None, None]), 0.0)
    q_decay = jnp.exp((pos[None, :] + 1.0) * log_g[:, None])
    k_decay = jnp.exp((c - 1.0 - pos[None, :]) * log_g[:, None])
    chunk_decay = jnp.exp(c * log_g)
    qd = jnp.broadcast_to(q_decay[:, :, None], (h_, c, LANES))
    kd = jnp.broadcast_to(k_decay[:, :, None], (h_, c, LANES))
    cd = jnp.broadcast_to(chunk_decay[:, None, None], (h_, SUBLANES, LANES))

    head_spec = lambda off: pl.BlockSpec((1, seq, RET_DIM), lambda bi, hi: (bi, 0, off + hi))
    const_spec = pl.BlockSpec((seq, LANES), lambda bi, hi: (0, 0))
    per_head = lambda r: pl.BlockSpec((1, r, LANES), lambda bi, hi: (hi, 0, 0))
    return pl.pallas_call(
        functools.partial(_retention_kernel, seq=seq),
        grid=(b, h_),
        in_specs=[head_spec(0), head_spec(h_), head_spec(2 * h_), head_spec(3 * h_),
                  const_spec, const_spec, per_head(c), per_head(c), per_head(c),
                  per_head(SUBLANES),
                  pl.BlockSpec((1, RET_DIM), lambda bi, hi: (0, hi))],
        out_specs=pl.BlockSpec((1, seq, RET_DIM), lambda bi, hi: (bi, 0, hi)),
        out_shape=jax.ShapeDtypeStruct((b, seq, RET_WIDTH), BF16),
        compiler_params=_cparams(2, 32),
        name="retention",
    )(proj, proj, proj, proj, cos2, sin2, inner, qd, kd, cd, gn.reshape(1, RET_WIDTH))


def _shift_rows(v, d, fill, row):
    return jnp.where(row >= d, pltpu.roll(v, d, 0), fill)


def _rglru_kernel(x_ref, y_ref, cw_ref, cb_ref, wa_ref, wx_ref, ba_ref, bx_ref, lam_ref,
                  o_ref, *, seq):
    x = x_ref[0]
    row = lax.broadcasted_iota(I32, x.shape, 0)
    cw = cw_ref[...]
    xc = cb_ref[...] + cw[0:1] * _shift_rows(x, 3, 0.0, row)
    xc = xc + cw[1:2] * _shift_rows(x, 2, 0.0, row)
    xc = xc + cw[2:3] * _shift_rows(x, 1, 0.0, row)
    xc = xc + cw[3:4] * x
    xb = xc.astype(BF16)
    r = jax.nn.sigmoid(_dot(xb, wa_ref[0].astype(BF16)) + ba_ref[...])
    ig = jax.nn.sigmoid(_dot(xb, wx_ref[0].astype(BF16)) + bx_ref[...])
    z = -lam_ref[...]
    softplus = jnp.maximum(z, 0.0) + jnp.log1p(jnp.exp(-jnp.abs(z)))
    log_a = -LRU_C * r * softplus
    a = jnp.exp(log_a)
    mult = jnp.sqrt(jnp.maximum(1.0 - jnp.exp(2.0 * log_a), 0.0))
    bb = mult * (ig * xc)
    d = 1
    while d < seq:
        a_sh = _shift_rows(a, d, 1.0, row)
        b_sh = _shift_rows(bb, d, 0.0, row)
        bb = bb + a * b_sh
        a = a * a_sh
        d *= 2
    yv = y_ref[0]
    gelu = 0.5 * yv * (1.0 + jnp.tanh(math.sqrt(2.0 / math.pi) * (yv + 0.044715 * (yv * yv * yv))))
    o_ref[0] = (bb * gelu).astype(o_ref.dtype)


def rglru(proj, conv_w, conv_b, wa, ba, wx, bx, lam, *, seq, x_off, y_off):
    b = proj.shape[0]
    nb = LRU_BLOCKS
    w = LRU_WIDTH
    blk = lambda off: pl.BlockSpec((1, seq, LANES), lambda bi, ni: (bi, 0, off + ni))
    vec = pl.BlockSpec((1, LANES), lambda bi, ni: (0, ni))
    mat = pl.BlockSpec((1, LRU_BLOCK_DIM, LRU_BLOCK_DIM), lambda bi, ni: (ni, 0, 0))
    return pl.pallas_call(
        functools.partial(_rglru_kernel, seq=seq),
        grid=(b, nb),
        in_specs=[blk(x_off), blk(y_off),
                  pl.BlockSpec((LRU_CONV, LANES), lambda bi, ni: (0, ni)),
                  vec, mat, mat, vec, vec, vec],
        out_specs=pl.BlockSpec((1, seq, LANES), lambda bi, ni: (bi, 0, ni)),
        out_shape=jax.ShapeDtypeStruct((b, seq, w), BF16),
        compiler_params=_cparams(2, 48),
        name="rglru",
    )(proj, proj, conv_w, conv_b.reshape(1, w), wa, wx, ba.reshape(1, w), bx.reshape(1, w),
      lam.reshape(1, w))


def _t5_bucket_np(rel):
    n = np.maximum(rel, 0)
    max_exact = REL_BUCKETS // 2
    nf = np.maximum(n, max_exact).astype(np.float64)
    large = max_exact + (np.log(nf / max_exact) / math.log(REL_MAX_DIST / max_exact)
                         * (REL_BUCKETS - max_exact)).astype(np.int32)
    large = np.minimum(large, REL_BUCKETS - 1)
    return np.where(n < max_exact, n, large).astype(np.int32)


def _dsa_kernel(rb_ref, bk_ref, q_ref, k_ref, v_ref, qi_ref, kwa_ref, kwq_ref, o_ref,
                kb_ref, vt_ref, kia_ref, kib_ref, qs_ref, qib_ref, key_ref, negm_ref,
                bias_ref, m_ref, l_ref, acc_ref, *, seq, tq, topk):
    bi = pl.program_id(0)
    i = pl.program_id(1)
    t0 = i * tq
    kc = LANES
    sc = 2 * LANES
    nch = seq // kc
    nsc = seq // sc
    nskip = topk // tq
    att_scale = ATT_DIM ** -0.5
    idx_scale = IDX_HEADS ** -0.5 * IDX_DIM ** -0.5

    @pl.when(i == 0)
    def _():
        for c in range(nch):
            kblk = k_ref[0, c * kc:(c + 1) * kc, :]
            vblk = v_ref[0, c * kc:(c + 1) * kc, :]
            for kh in range(KV_HEADS):
                kb_ref[kh, c] = kblk[:, kh * ATT_DIM:(kh + 1) * ATT_DIM].astype(BF16)
                vt_ref[kh, c] = vblk[:, kh * ATT_DIM:(kh + 1) * ATT_DIM].T.astype(BF16)
        lane = lax.broadcasted_iota(I32, (sc, LANES), 1)
        for c2 in range(nsc):
            kw = kwa_ref[0, c2 * sc:(c2 + 1) * sc, :]
            ka = jnp.where(lane < IDX_DIM, kw, 0.0)
            kia_ref[c2] = ka.astype(BF16)
            kib_ref[c2] = pltpu.roll(ka, IDX_DIM, 1).astype(BF16)

    @pl.when((bi == 0) & (i == 0))
    def _():
        bias_ref[...] = jnp.zeros(bias_ref.shape, F32)

        def fill(bu, c):
            for r in range(2):
                hit = bk_ref[r] == bu
                for h in range(ATT_HEADS):
                    kh, g = divmod(h, GROUP)
                    val = rb_ref[bu, h] - rb_ref[REL_BUCKETS - 1, h]
                    cur = bias_ref[kh, r, :, g * tq:(g + 1) * tq]
                    bias_ref[kh, r, :, g * tq:(g + 1) * tq] = jnp.where(hit, val, cur)
            return c

        lax.fori_loop(0, REL_BUCKETS, fill, 0)

    for h in range(ATT_HEADS):
        kh, g = divmod(h, GROUP)
        qs_ref[kh, g * tq:(g + 1) * tq, :] = q_ref[0, :, h * ATT_DIM:(h + 1) * ATT_DIM].astype(BF16)

    row_s = lax.broadcasted_iota(I32, (sc, tq), 0)
    tidx = t0 + lax.broadcasted_iota(I32, (sc, tq), 1)

    @pl.when(i < nskip)
    def _():
        for c2 in range((nskip * tq + sc - 1) // sc):
            sidx = c2 * sc + row_s
            nm = jnp.where(sidx <= tidx, 0.0, NEG_INF)
            negm_ref[2 * c2] = nm[:kc]
            negm_ref[2 * c2 + 1] = nm[kc:]

    @pl.when(i >= nskip)
    def _():
        nsel = (t0 + tq + sc - 1) // sc
        qib_ref[...] = qi_ref[0].astype(BF16)
        wt = kwq_ref[0].T * idx_scale

        def score_chunk(c2, carry):
            ka = kia_ref[c2]
            kb_ = kib_ref[c2]
            acc = jnp.zeros((sc, tq), F32)
            for p in range(IDX_HEADS // 2):
                qp = qib_ref[:, p * LANES:(p + 1) * LANES]
                w0 = wt[IDX_DIM + 2 * p:IDX_DIM + 2 * p + 1, :]
                w1 = wt[IDX_DIM + 2 * p + 1:IDX_DIM + 2 * p + 2, :]
                acc = acc + jnp.maximum(_dot_nt(ka, qp), 0.0) * w0
                acc = acc + jnp.maximum(_dot_nt(kb_, qp), 0.0) * w1
            sidx = c2 * sc + row_s
            score = jnp.where(sidx <= tidx, acc + 0.0, NEG_INF)
            bits = lax.bitcast_convert_type(score, I32)
            key_ref[c2] = jnp.where(bits < 0, bits ^ jnp.int32(0x7FFFFFFF), bits)
            return carry

        lax.fori_loop(0, nsel, score_chunk, 0)

        def count(pred):
            def body(c2, acc8):
                hit = pred(key_ref[c2], c2 * sc + row_s).astype(I32)
                return acc8 + hit.reshape(sc // SUBLANES, SUBLANES, tq).sum(axis=0)
            acc8 = lax.fori_loop(0, nsel, body, jnp.zeros((SUBLANES, tq), I32))
            return acc8.sum(axis=0, keepdims=True)

        def bit_step(it, thr):
            cand = thr + lax.shift_left(jnp.int32(1), 31 - it)
            cnt = count(lambda kk, sidx: kk >= cand)
            return jnp.where(cnt >= topk, cand, thr)

        thr = lax.fori_loop(0, 32, bit_step, jnp.full((1, tq), INT_MIN, I32))
        need = topk - count(lambda kk, sidx: kk > thr)

        nbits = int(seq).bit_length()

        def tie_step(it, pos):
            cand = pos + lax.shift_left(jnp.int32(1), nbits - 1 - it)
            cnt = count(lambda kk, sidx: (kk == thr) & (sidx < cand))
            return jnp.where(cnt <= need, cand, pos)

        pos = lax.fori_loop(0, nbits, tie_step, jnp.zeros((1, tq), I32))

        def mask_chunk(c2, carry):
            kk = key_ref[c2]
            sidx = c2 * sc + row_s
            sel = (kk > thr) | ((kk == thr) & (sidx < pos))
            nm = jnp.where(sel & (sidx <= tidx), 0.0, NEG_INF)
            negm_ref[2 * c2] = nm[:kc]
            negm_ref[2 * c2 + 1] = nm[kc:]
            return carry

        lax.fori_loop(0, nsel, mask_chunk, 0)

    for kh in range(KV_HEADS):
        qs = qs_ref[kh]
        m_ref[...] = jnp.full(m_ref.shape, -jnp.inf, F32)
        l_ref[...] = jnp.zeros(l_ref.shape, F32)
        acc_ref[...] = jnp.zeros(acc_ref.shape, F32)

        def attend(c, bias):
            s = _dot_nt(kb_ref[kh, c], qs) * att_scale
            nm = negm_ref[c]
            s = s + jnp.concatenate([nm] * GROUP, axis=1)
            if bias is not None:
                s = s + bias
            m_old = m_ref[...]
            m_new = jnp.maximum(m_old, jnp.max(s, axis=0, keepdims=True))
            alpha = jnp.exp(m_old - m_new)
            p = jnp.exp(s - m_new)
            l_ref[...] = alpha * l_ref[...] + jnp.sum(p, axis=0, keepdims=True)
            acc_ref[...] = alpha * acc_ref[...] + _dot(vt_ref[kh, c], p.astype(BF16))
            m_ref[...] = m_new

        def far_body(c, carry):
            attend(c, None)
            return carry

        lax.fori_loop(0, jnp.maximum(i - 1, 0), far_body, 0)

        @pl.when(i >= 1)
        def _():
            attend(i - 1, bias_ref[kh, 0])

        attend(i, bias_ref[kh, 1])

        out_t = acc_ref[...] / l_ref[...]
        for g in range(GROUP):
            h = kh * GROUP + g
            o_ref[0, :, h * ATT_DIM:(h + 1) * ATT_DIM] = (
                out_t[:, g * tq:(g + 1) * tq].T.astype(o_ref.dtype))


def dsa_attention(proj, rel_bias, *, seq, tq=LANES):
    assert tq == LANES
    b = proj.shape[0]
    topk = min(TOPK_MAX, seq // 4)
    assert topk % tq == 0 and seq % (2 * LANES) == 0
    o_q = ATT_HEADS * ATT_DIM
    o_kv = KV_HEADS * ATT_DIM
    o_qi = IDX_HEADS * IDX_DIM
    nch = seq // LANES
    jj, ii = np.meshgrid(np.arange(tq), np.arange(tq), indexing="ij")
    buckets = jnp.asarray(np.stack([_t5_bucket_np(tq + ii - jj), _t5_bucket_np(ii - jj)]))
    in_specs = [
        pl.BlockSpec(memory_space=pltpu.SMEM),
        pl.BlockSpec((2, tq, tq), lambda bi, i: (0, 0, 0)),
        pl.BlockSpec((1, tq, o_q), lambda bi, i: (bi, i, 0)),
        pl.BlockSpec((1, seq, o_kv), lambda bi, i: (bi, 0, o_q // o_kv)),
        pl.BlockSpec((1, seq, o_kv), lambda bi, i: (bi, 0, o_q // o_kv + 1)),
        pl.BlockSpec((1, tq, o_qi), lambda bi, i: (bi, i, (o_q + 2 * o_kv) // o_qi)),
        pl.BlockSpec((1, seq, LANES), lambda bi, i: (bi, 0, (o_q + 2 * o_kv + o_qi) // LANES)),
        pl.BlockSpec((1, tq, LANES), lambda bi, i: (bi, i, (o_q + 2 * o_kv + o_qi) // LANES)),
    ]
    scratch = [
        pltpu.VMEM((KV_HEADS, nch, LANES, ATT_DIM), BF16),
        pltpu.VMEM((KV_HEADS, nch, ATT_DIM, LANES), BF16),
        pltpu.VMEM((seq // (2 * LANES), 2 * LANES, LANES), BF16),
        pltpu.VMEM((seq // (2 * LANES), 2 * LANES, LANES), BF16),
        pltpu.VMEM((KV_HEADS, GROUP * tq, ATT_DIM), BF16),
        pltpu.VMEM((tq, o_qi), BF16),
        pltpu.VMEM((seq // (2 * LANES), 2 * LANES, tq), I32),
        pltpu.VMEM((nch, LANES, tq), F32),
        pltpu.VMEM((KV_HEADS, 2, LANES, GROUP * tq), F32),
        pltpu.VMEM((1, GROUP * tq), F32),
        pltpu.VMEM((1, GROUP * tq), F32),
        pltpu.VMEM((ATT_DIM, GROUP * tq), F32),
    ]
    return pl.pallas_call(
        functools.partial(_dsa_kernel, seq=seq, tq=tq, topk=topk),
        grid=(b, seq // tq),
        in_specs=in_specs,
        out_specs=pl.BlockSpec((1, tq, o_q), lambda bi, i: (bi, i, 0)),
        out_shape=jax.ShapeDtypeStruct((b, seq, o_q), BF16),
        scratch_shapes=scratch,
        compiler_params=_cparams(2, 48),
        name="dsa_attention",
    )(rel_bias, buckets, proj, proj, proj, proj, proj, proj)


def kernel(x, norm_mix, norm_ffn, e_w_in, e_ret_gn, e_conv_w, e_conv_b, e_gate_a_w, e_gate_a_b,
           e_gate_x_w, e_gate_x_b, e_lambda, e_w_out, o_w_in, o_w_out, rel_bias,
           ffn_w_gu, ffn_conv_w, ffn_conv_b, ffn_w_down, final_norm):
    b, s, d = x.shape
    m = b * s
    tm = 512
    x2 = x.reshape(m, d)

    e_in = e_w_in.shape[-1]
    proj = norm_matmul(x2, norm_mix[0], e_w_in[0].astype(BF16), tm=1024, tn=512)
    proj = proj.reshape(b, s, e_in)
    ret = retention(proj, e_ret_gn[0], seq=s)
    lru = rglru(proj, e_conv_w[0], e_conv_b[0], e_gate_a_w[0], e_gate_a_b[0], e_gate_x_w[0],
                e_gate_x_b[0], e_lambda[0], seq=s,
                x_off=4 * RET_HEADS, y_off=4 * RET_HEADS + LRU_BLOCKS)
    x2 = matmul_residual(x2, ret.reshape(m, RET_WIDTH), 0, lru.reshape(m, LRU_WIDTH), 0,
                         e_w_out[0].astype(BF16), tm=tm)
    x2 = conv_ffn(x2, norm_ffn[0], ffn_w_gu[0].astype(BF16), ffn_conv_w[0], ffn_conv_b[0],
                  ffn_w_down[0].astype(BF16), final_norm, seq=s, tm=tm, tf=512,
                  final_norm=False)

    o_in = o_w_in.shape[-1]
    n_pad = -o_in % 512
    w_in = jnp.pad(o_w_in[0].astype(BF16), ((0, 0), (0, n_pad)))
    proj = norm_matmul(x2, norm_mix[1], w_in, tm=1024, tn=512)
    attn = dsa_attention(proj.reshape(b, s, o_in + n_pad), rel_bias, seq=s)
    attn = attn.reshape(m, ATT_HEADS * ATT_DIM)
    x2 = matmul_residual(x2, attn, 0, attn, 1, o_w_out[0].astype(BF16), tm=tm)
    x2 = conv_ffn(x2, norm_ffn[1], ffn_w_gu[1].astype(BF16), ffn_conv_w[1], ffn_conv_b[1],
                  ffn_w_down[1].astype(BF16), final_norm, seq=s, tm=tm, tf=512,
                  final_norm=True)
    return x2.reshape(b, s, d)
```

```python
import functools
import math

import numpy as np
import jax
import jax.numpy as jnp
from jax import lax
from jax.experimental import pallas as pl
from jax.experimental.pallas import tpu as pltpu

F32 = jnp.float32
BF16 = jnp.bfloat16
I32 = jnp.int32

EPS = 1e-6
NEG_INF = -1e30
ROPE_BASE = 10000.0

RET_HEADS = 8
RET_DIM = 128
RET_WIDTH = RET_HEADS * RET_DIM
RET_CHUNK = 128
LRU_BLOCKS = 8
LRU_BLOCK_DIM = 128
LRU_WIDTH = LRU_BLOCKS * LRU_BLOCK_DIM
LRU_CONV = 4
LRU_C = 8.0

ATT_HEADS = 16
ATT_DIM = 128
KV_HEADS = 4
GROUP = ATT_HEADS // KV_HEADS
IDX_HEADS = 16
IDX_DIM = 64
TOPK_MAX = 256
REL_BUCKETS = 32
REL_MAX_DIST = 128
FFN_CONV = 3

LANES = 128
SUBLANES = 8
MIB = 2 ** 20
INT_MIN = -2 ** 31


def _cparams(n_axes, vmem_mib):
    return pltpu.CompilerParams(
        dimension_semantics=("arbitrary",) * n_axes,
        vmem_limit_bytes=int(vmem_mib * MIB))


def _dot(a, b):
    return jnp.dot(a, b, preferred_element_type=F32)


def _dot_nt(a, b):
    return lax.dot_general(a, b, (((1,), (1,)), ((), ())), preferred_element_type=F32)


def _dot_tn(a, b):
    return lax.dot_general(a, b, (((0,), (0,)), ((), ())), preferred_element_type=F32)


def _rms_rows(x, g):
    ms = jnp.mean(x * x, axis=-1, keepdims=True)
    return x * lax.rsqrt(ms + EPS) * g


def _rms_to_ref(x_ref, g_ref, h_ref, row_chunk):
    g = g_ref[...]

    def body(r, c):
        sl = pl.ds(pl.multiple_of(r * row_chunk, row_chunk), row_chunk)
        h_ref[sl, :] = _rms_rows(x_ref[sl, :], g).astype(h_ref.dtype)
        return c

    lax.fori_loop(0, x_ref.shape[0] // row_chunk, body, 0)


def _norm_matmul_kernel(x_ref, g_ref, w_ref, o_ref, h_ref, *, row_chunk):
    @pl.when(pl.program_id(1) == 0)
    def _():
        _rms_to_ref(x_ref, g_ref, h_ref, row_chunk)

    o_ref[...] = _dot(h_ref[...], w_ref[...]).astype(o_ref.dtype)


def norm_matmul(x, g, w, *, tm, tn, out_dtype=F32, row_chunk=128):
    m, d = x.shape
    n = w.shape[1]
    vmem = (2 * tm * d * 4 + 2 * d * tn * 2 + 2 * tm * tn * 4 + tm * d * 2) / MIB + 8
    return pl.pallas_call(
        functools.partial(_norm_matmul_kernel, row_chunk=row_chunk),
        grid=(m // tm, n // tn),
        in_specs=[
            pl.BlockSpec((tm, d), lambda i, j: (i, 0)),
            pl.BlockSpec((1, d), lambda i, j: (0, 0)),
            pl.BlockSpec((d, tn), lambda i, j: (0, j)),
        ],
        out_specs=pl.BlockSpec((tm, tn), lambda i, j: (i, j)),
        out_shape=jax.ShapeDtypeStruct((m, n), out_dtype),
        scratch_shapes=[pltpu.VMEM((tm, d), BF16)],
        compiler_params=_cparams(2, vmem),
        name="norm_matmul",
    )(x, g.reshape(1, d), w)


def _matmul_residual_kernel(x_ref, a1_ref, a2_ref, w1_ref, w2_ref, o_ref):
    o_ref[...] = x_ref[...] + _dot(a1_ref[...], w1_ref[...]) + _dot(a2_ref[...], w2_ref[...])


def matmul_residual(x, a1, c1, a2, c2, w, *, tm):
    m, n = x.shape
    k1 = w.shape[0] // 2
    vmem = (4 * tm * n * 4 + 4 * tm * k1 * 2 + 4 * k1 * n * 2) / MIB + 8
    return pl.pallas_call(
        _matmul_residual_kernel,
        grid=(m // tm,),
        in_specs=[
            pl.BlockSpec((tm, n), lambda i: (i, 0)),
            pl.BlockSpec((tm, k1), lambda i: (i, c1)),
            pl.BlockSpec((tm, k1), lambda i: (i, c2)),
            pl.BlockSpec((k1, n), lambda i: (0, 0)),
            pl.BlockSpec((k1, n), lambda i: (1, 0)),
        ],
        out_specs=pl.BlockSpec((tm, n), lambda i: (i, 0)),
        out_shape=jax.ShapeDtypeStruct((m, n), F32),
        compiler_params=_cparams(1, vmem),
        name="matmul_residual",
    )(x, a1, a2, w, w)


def _ffn_kernel(x_ref, g_ref, wg_ref, wu_ref, cw_ref, cb_ref, wd_ref, gf_ref, o_ref,
                h_ref, carry_ref, *, tm, seq, row_chunk, final_norm):
    i = pl.program_id(0)
    j = pl.program_id(1)
    nj = pl.num_programs(1)

    @pl.when(j == 0)
    def _():
        _rms_to_ref(x_ref, g_ref, h_ref, row_chunk)
        o_ref[...] = x_ref[...]

    h = h_ref[...]
    g = _dot(h, wg_ref[...])
    u = _dot(h, wu_ref[...])

    seq_start = (i * tm) % seq == 0
    prev = jnp.where(seq_start, 0.0, carry_ref[j])
    carry_ref[j] = g[tm - SUBLANES:, :]
    row = lax.broadcasted_iota(I32, (SUBLANES, g.shape[1]), 0)
    g1 = pltpu.roll(g, 1, 0)
    g2 = pltpu.roll(g, 2, 0)
    g1_top = jnp.where(row == 0, prev[7:8, :], g1[:SUBLANES])
    g2_top = jnp.where(row == 0, prev[6:7, :], jnp.where(row == 1, prev[7:8, :], g2[:SUBLANES]))
    g1 = jnp.concatenate([g1_top, g1[SUBLANES:]], axis=0)
    g2 = jnp.concatenate([g2_top, g2[SUBLANES:]], axis=0)
    cw = cw_ref[...]
    gc = cb_ref[...] + cw[0:1] * g2 + cw[1:2] * g1 + cw[2:3] * g
    act = (gc * jax.nn.sigmoid(gc) * u).astype(BF16)
    o_ref[...] += _dot(act, wd_ref[...])

    if final_norm:
        @pl.when(j == nj - 1)
        def _():
            _rms_to_ref(o_ref, gf_ref, o_ref, row_chunk)


def conv_ffn(x, g, w_gu, conv_w, conv_b, w_down, g_final, *, seq, tm, tf,
             final_norm, row_chunk=128):
    m, d = x.shape
    f = w_down.shape[0]
    nf = f // tf
    vmem = (4 * tm * d * 4 + tm * d * 2 + 4 * d * tf * 2 + 2 * tf * d * 2
            + 6 * tm * tf * 4) / MIB + 6
    return pl.pallas_call(
        functools.partial(_ffn_kernel, tm=tm, seq=seq, row_chunk=row_chunk,
                          final_norm=final_norm),
        grid=(m // tm, nf),
        in_specs=[
            pl.BlockSpec((tm, d), lambda i, j: (i, 0)),
            pl.BlockSpec((1, d), lambda i, j: (0, 0)),
            pl.BlockSpec((d, tf), lambda i, j: (0, j)),
            pl.BlockSpec((d, tf), lambda i, j: (0, j + nf)),
            pl.BlockSpec((FFN_CONV, tf), lambda i, j: (0, j)),
            pl.BlockSpec((1, tf), lambda i, j: (0, j)),
            pl.BlockSpec((tf, d), lambda i, j: (j, 0)),
            pl.BlockSpec((1, d), lambda i, j: (0, 0)),
        ],
        out_specs=pl.BlockSpec((tm, d), lambda i, j: (i, 0)),
        out_shape=jax.ShapeDtypeStruct((m, d), F32),
        scratch_shapes=[pltpu.VMEM((tm, d), BF16),
                        pltpu.VMEM((nf, SUBLANES, tf), F32)],
        compiler_params=_cparams(2, vmem),
        name="conv_ffn",
    )(x, g.reshape(1, d), w_gu, w_gu, conv_w, conv_b.reshape(1, f), w_down,
      g_final.reshape(1, d))


def _retention_kernel(q_ref, k_ref, v_ref, gate_ref, cos_ref, sin_ref, dec_ref, qd_ref,
                      kd_ref, cd_ref, gn_ref, o_ref, state_ref, *, seq, hp):
    c = RET_CHUNK
    scale = RET_DIM ** -0.5
    state_ref[...] = jnp.zeros(state_ref.shape, F32)

    def chunk(n, carry):
        sl = pl.ds(pl.multiple_of(n * c, c), c)
        cos = cos_ref[sl, :]
        sin = sin_ref[sl, :]
        for hh in range(hp):
            cols = slice(hh * RET_DIM, (hh + 1) * RET_DIM)
            q = q_ref[0, sl, cols]
            k = k_ref[0, sl, cols]
            v = v_ref[0, sl, cols].astype(BF16)
            state = state_ref[hh]
            qr = q * cos + pltpu.roll(q, RET_DIM // 2, 1) * sin
            kr = (k * cos + pltpu.roll(k, RET_DIM // 2, 1) * sin) * scale
            scores = _dot_nt(qr.astype(BF16), kr.astype(BF16)) * dec_ref[hh]
            y = _dot(scores.astype(BF16), v)
            y = y + _dot((qr * qd_ref[hh]).astype(BF16), state.astype(BF16))
            kv = _dot_tn((kr * kd_ref[hh]).astype(BF16), v)
            state_ref[hh] = cd_ref[hh][0:1, :] * state + kv
            y = y * lax.rsqrt(jnp.mean(y * y, axis=-1, keepdims=True) + EPS)
            y = y * gn_ref[:, cols]
            gate = gate_ref[0, sl, cols]
            o_ref[0, sl, cols] = (y * (gate * jax.nn.sigmoid(gate))).astype(o_ref.dtype)
        return carry

    lax.fori_loop(0, seq // c, chunk, 0)


def retention(proj, gn, *, seq, hp=4):
    b = proj.shape[0]
    h_ = RET_HEADS
    c = RET_CHUNK
    half = RET_DIM // 2
    freqs = ROPE_BASE ** (-jnp.arange(half, dtype=F32) / half)
    ang = jnp.arange(seq, dtype=F32)[:, None] * freqs[None, :]
    cos2 = jnp.concatenate([jnp.cos(ang), jnp.cos(ang)], axis=-1)
    sin2 = jnp.concatenate([-jnp.sin(ang), jnp.sin(ang)], axis=-1)
    log_g = jnp.log1p(-jnp.exp2(-5.0 - jnp.arange(h_, dtype=F32)))
    pos = jnp.arange(c, dtype=F32)
    diff = pos[:, None] - pos[None, :]
    inner = jnp.where(diff[None] >= 0,
                      jnp.exp(jnp.maximum(diff, 0.0)[None] * log_g[:, None, None]), 0.0)
    q_decay = jnp.exp((pos[None, :] + 1.0) * log_g[:, None])
    k_decay = jnp.exp((c - 1.0 - pos[None, :]) * log_g[:, None])
    chunk_decay = jnp.exp(c * log_g)
    qd = jnp.broadcast_to(q_decay[:, :, None], (h_, c, LANES))
    kd = jnp.broadcast_to(k_decay[:, :, None], (h_, c, LANES))
    cd = jnp.broadcast_to(chunk_decay[:, None, None], (h_, SUBLANES, LANES))

    ng = h_ // hp
    wide = hp * RET_DIM
    head_spec = lambda sec: pl.BlockSpec((1, seq, wide), lambda bi, hi: (bi, 0, sec * ng + hi))
    const_spec = pl.BlockSpec((seq, LANES), lambda bi, hi: (0, 0))
    per_head = lambda r: pl.BlockSpec((hp, r, LANES), lambda bi, hi: (hi, 0, 0))
    vmem = (2 * 4 * seq * wide * 4 + 2 * seq * wide * 2 + 4 * seq * LANES * 4) / MIB + 6
    return pl.pallas_call(
        functools.partial(_retention_kernel, seq=seq, hp=hp),
        grid=(b, ng),
        in_specs=[head_spec(0), head_spec(1), head_spec(2), head_spec(3),
                  const_spec, const_spec, per_head(c), per_head(c), per_head(c),
                  per_head(SUBLANES),
                  pl.BlockSpec((1, wide), lambda bi, hi: (0, hi))],
        out_specs=pl.BlockSpec((1, seq, wide), lambda bi, hi: (bi, 0, hi)),
        out_shape=jax.ShapeDtypeStruct((b, seq, RET_WIDTH), BF16),
        scratch_shapes=[pltpu.VMEM((hp, RET_DIM, RET_DIM), F32)],
        compiler_params=_cparams(2, vmem),
        name="retention",
    )(proj, proj, proj, proj, cos2, sin2, inner, qd, kd, cd, gn.reshape(1, RET_WIDTH))


def _shift_rows(v, d, fill, row):
    return jnp.where(row >= d, pltpu.roll(v, d, 0), fill)


def _rglru_kernel(x_ref, y_ref, cw_ref, cb_ref, wa_ref, wx_ref, ba_ref, bx_ref, lam_ref,
                  o_ref, *, seq):
    x = x_ref[0]
    row = lax.broadcasted_iota(I32, x.shape, 0)
    cw = cw_ref[...]
    xc = cb_ref[...] + cw[0:1] * _shift_rows(x, 3, 0.0, row)
    xc = xc + cw[1:2] * _shift_rows(x, 2, 0.0, row)
    xc = xc + cw[2:3] * _shift_rows(x, 1, 0.0, row)
    xc = xc + cw[3:4] * x
    xb = xc.astype(BF16)
    r = jax.nn.sigmoid(_dot(xb, wa_ref[0].astype(BF16)) + ba_ref[...])
    ig = jax.nn.sigmoid(_dot(xb, wx_ref[0].astype(BF16)) + bx_ref[...])
    z = -lam_ref[...]
    softplus = jnp.maximum(z, 0.0) + jnp.log1p(jnp.exp(-jnp.abs(z)))
    log_a = -LRU_C * r * softplus
    a = jnp.exp(log_a)
    mult = jnp.sqrt(jnp.maximum(1.0 - jnp.exp(2.0 * log_a), 0.0))
    bb = mult * (ig * xc)
    d = 1
    while d < seq:
        a_sh = _shift_rows(a, d, 1.0, row)
        b_sh = _shift_rows(bb, d, 0.0, row)
        bb = bb + a * b_sh
        a = a * a_sh
        d *= 2
    yv = y_ref[0]
    gelu = 0.5 * yv * (1.0 + jnp.tanh(math.sqrt(2.0 / math.pi) * (yv + 0.044715 * (yv * yv * yv))))
    o_ref[0] = (bb * gelu).astype(o_ref.dtype)


def rglru(proj, conv_w, conv_b, wa, ba, wx, bx, lam, *, seq, x_off, y_off):
    b = proj.shape[0]
    nb = LRU_BLOCKS
    w = LRU_WIDTH
    blk = lambda off: pl.BlockSpec((1, seq, LANES), lambda bi, ni: (bi, 0, off + ni))
    vec = pl.BlockSpec((1, LANES), lambda bi, ni: (0, ni))
    mat = pl.BlockSpec((1, LRU_BLOCK_DIM, LRU_BLOCK_DIM), lambda bi, ni: (ni, 0, 0))
    return pl.pallas_call(
        functools.partial(_rglru_kernel, seq=seq),
        grid=(b, nb),
        in_specs=[blk(x_off), blk(y_off),
                  pl.BlockSpec((LRU_CONV, LANES), lambda bi, ni: (0, ni)),
                  vec, mat, mat, vec, vec, vec],
        out_specs=pl.BlockSpec((1, seq, LANES), lambda bi, ni: (bi, 0, ni)),
        out_shape=jax.ShapeDtypeStruct((b, seq, w), BF16),
        compiler_params=_cparams(2, 48),
        name="rglru",
    )(proj, proj, conv_w, conv_b.reshape(1, w), wa, wx, ba.reshape(1, w), bx.reshape(1, w),
      lam.reshape(1, w))


def _t5_bucket_np(rel):
    n = np.maximum(rel, 0)
    max_exact = REL_BUCKETS // 2
    nf = np.maximum(n, max_exact).astype(np.float64)
    large = max_exact + (np.log(nf / max_exact) / math.log(REL_MAX_DIST / max_exact)
                         * (REL_BUCKETS - max_exact)).astype(np.int32)
    large = np.minimum(large, REL_BUCKETS - 1)
    return np.where(n < max_exact, n, large).astype(np.int32)


def _dsa_kernel(rb_ref, bk_ref, q_ref, k_ref, v_ref, qi_ref, kwa_ref, kwq_ref, o_ref,
                kb_ref, vt_ref, kia_ref, kib_ref, qs_ref, qst_ref, key_ref, negm_ref,
                bias_ref, s_ref, m_ref, acc_ref, pos_ref, *, seq, tq, topk):
    bi = pl.program_id(0)
    i = pl.program_id(1)
    t0 = i * tq
    sc = 2 * LANES
    nsc = seq // sc
    gt = GROUP * tq
    nskip = topk // tq
    att_scale = ATT_DIM ** -0.5
    idx_scale = IDX_HEADS ** -0.5 * IDX_DIM ** -0.5
    nsel = (t0 + tq + sc - 1) // sc

    @pl.when(i == 0)
    def _():
        lane = lax.broadcasted_iota(I32, (sc, LANES), 1)
        for u in range(nsc):
            kblk = k_ref[0, u * sc:(u + 1) * sc, :]
            vblk = v_ref[0, u * sc:(u + 1) * sc, :]
            for kh in range(KV_HEADS):
                kb_ref[kh, u] = kblk[:, kh * ATT_DIM:(kh + 1) * ATT_DIM].astype(BF16)
                vt_ref[kh, u] = vblk[:, kh * ATT_DIM:(kh + 1) * ATT_DIM].T.astype(BF16)
            kw = kwa_ref[0, u * sc:(u + 1) * sc, :]
            ka = jnp.where(lane < IDX_DIM, kw, 0.0)
            kia_ref[u] = ka.astype(BF16)
            kib_ref[u] = pltpu.roll(ka, IDX_DIM, 1).astype(BF16)

    @pl.when((bi == 0) & (i == 0))
    def _():
        bias_ref[...] = jnp.zeros(bias_ref.shape, F32)

        def fill(bu, c):
            for r in range(2):
                hit = bk_ref[r] == bu
                for h in range(ATT_HEADS):
                    kh, g = divmod(h, GROUP)
                    val = rb_ref[bu, h] - rb_ref[REL_BUCKETS - 1, h]
                    rows = slice((2 + r) * LANES, (3 + r) * LANES)
                    cur = bias_ref[kh, rows, g * tq:(g + 1) * tq]
                    bias_ref[kh, rows, g * tq:(g + 1) * tq] = jnp.where(hit, val, cur)
            return c

        lax.fori_loop(0, REL_BUCKETS, fill, 0)

    for h in range(ATT_HEADS):
        kh, g = divmod(h, GROUP)
        qs_ref[kh, g * tq:(g + 1) * tq, :] = q_ref[0, :, h * ATT_DIM:(h + 1) * ATT_DIM].astype(BF16)

    row_s = lax.broadcasted_iota(I32, (sc, tq), 0)
    tidx = t0 + lax.broadcasted_iota(I32, (sc, tq), 1)

    @pl.when(i < nskip)
    def _():
        for u in range((nskip * tq + sc - 1) // sc):
            negm_ref[u] = jnp.where(u * sc + row_s <= tidx, 0.0, NEG_INF)

    @pl.when(i >= nskip)
    def _():
        for pp in range(IDX_HEADS // 4):
            for r in range(2):
                cols = slice((2 * pp + r) * LANES, (2 * pp + r + 1) * LANES)
                qst_ref[pp, r * tq:(r + 1) * tq, :] = qi_ref[0, :, cols].astype(BF16)
        wt = kwq_ref[0].T * idx_scale

        def head_w(h):
            return wt[IDX_DIM + h:IDX_DIM + h + 1, :]

        def score_unit(u, carry):
            ka = kia_ref[u]
            kb_ = kib_ref[u]
            acc = jnp.zeros((sc, tq), F32)
            for pp in range(IDX_HEADS // 4):
                q2 = qst_ref[pp]
                sa = jnp.maximum(_dot_nt(ka, q2), 0.0)
                sb = jnp.maximum(_dot_nt(kb_, q2), 0.0)
                acc = acc + sa[:, :tq] * head_w(4 * pp) + sb[:, :tq] * head_w(4 * pp + 1)
                acc = acc + sa[:, tq:] * head_w(4 * pp + 2) + sb[:, tq:] * head_w(4 * pp + 3)
            score = jnp.where(u * sc + row_s <= tidx, acc + 0.0, NEG_INF)
            bits = lax.bitcast_convert_type(score, I32)
            key_ref[u] = jnp.where(bits < 0, bits ^ jnp.int32(0x7FFFFFFF), bits)
            return carry

        lax.fori_loop(0, nsel, score_unit, 0)

        def count(pred):
            def body(c2, acc8):
                hit = pred(key_ref[c2], c2 * sc + row_s).astype(I32)
                return acc8 + hit.reshape(sc // SUBLANES, SUBLANES, tq).sum(axis=0)
            acc8 = lax.fori_loop(0, nsel, body, jnp.zeros((SUBLANES, tq), I32))
            return acc8.sum(axis=0, keepdims=True)

        def bit_step(it, thr):
            cand = thr + lax.shift_left(jnp.int32(1), 31 - it)
            cnt = count(lambda kk, sidx: kk >= cand)
            return jnp.where(cnt >= topk, cand, thr)

        thr = lax.fori_loop(0, 32, bit_step, jnp.full((1, tq), INT_MIN, I32))
        need = topk - count(lambda kk, sidx: kk > thr)
        n_tie = count(lambda kk, sidx: kk == thr)

        nbits = int(seq).bit_length()
        pos_ref[...] = jnp.full((1, tq), 2 ** nbits, I32)

        @pl.when(jnp.max(jnp.where(n_tie != need, 1, 0)) > 0)
        def _():
            def tie_step(it, pos):
                cand = pos + lax.shift_left(jnp.int32(1), nbits - 1 - it)
                cnt = count(lambda kk, sidx: (kk == thr) & (sidx < cand))
                return jnp.where(cnt <= need, cand, pos)

            pos_ref[...] = lax.fori_loop(0, nbits, tie_step, jnp.zeros((1, tq), I32))

        pos = pos_ref[...]

        def mask_unit(u, carry):
            kk = key_ref[u]
            sidx = u * sc + row_s
            sel = (kk > thr) | ((kk == thr) & (sidx < pos))
            negm_ref[u] = jnp.where(sel & (sidx <= tidx), 0.0, NEG_INF)
            return carry

        lax.fori_loop(0, nsel, mask_unit, 0)

    nfar = jnp.maximum((i - 1) // 2, 0)
    part = (sc // SUBLANES, SUBLANES, gt)

    def logits_unit(u, mparts, with_bias):
        nm = negm_ref[u]
        nm = jnp.concatenate([nm] * GROUP, axis=1)
        out = []
        for kh in range(KV_HEADS):
            s = _dot_nt(kb_ref[kh, u], qs_ref[kh]) * att_scale + nm
            if with_bias:
                start = pl.multiple_of((2 * u - i + 3) * LANES, LANES)
                s = s + bias_ref[kh, pl.ds(start, sc), :]
            s_ref[kh, u] = s
            out.append(jnp.maximum(mparts[kh], s.reshape(part).max(axis=0)))
        return tuple(out)

    mparts = tuple(jnp.full((SUBLANES, gt), -jnp.inf, F32) for _ in range(KV_HEADS))
    mparts = lax.fori_loop(0, nfar, lambda u, c: logits_unit(u, c, False), mparts)
    mparts = lax.fori_loop(nfar, nsel, lambda u, c: logits_unit(u, c, True), mparts)
    for kh in range(KV_HEADS):
        m_ref[kh] = jnp.broadcast_to(mparts[kh].max(axis=0, keepdims=True), (SUBLANES, gt))
    acc_ref[...] = jnp.zeros(acc_ref.shape, F32)

    def probs_unit(u, lparts):
        out = []
        for kh in range(KV_HEADS):
            p = jnp.exp(s_ref[kh, u].reshape(part) - m_ref[kh][None])
            out.append(lparts[kh] + p.sum(axis=0))
            acc_ref[kh] += _dot(vt_ref[kh, u], p.reshape(sc, gt).astype(BF16))
        return tuple(out)

    lparts = tuple(jnp.zeros((SUBLANES, gt), F32) for _ in range(KV_HEADS))
    lparts = lax.fori_loop(0, nsel, probs_unit, lparts)
    for kh in range(KV_HEADS):
        out_t = acc_ref[kh] / lparts[kh].sum(axis=0, keepdims=True)
        for g in range(GROUP):
            h = kh * GROUP + g
            o_ref[0, :, h * ATT_DIM:(h + 1) * ATT_DIM] = (
                out_t[:, g * tq:(g + 1) * tq].T.astype(o_ref.dtype))


def dsa_attention(proj, rel_bias, *, seq, tq=LANES):
    assert tq == LANES
    b = proj.shape[0]
    topk = min(TOPK_MAX, seq // 4)
    assert topk % tq == 0 and seq % (2 * LANES) == 0
    o_q = ATT_HEADS * ATT_DIM
    o_kv = KV_HEADS * ATT_DIM
    o_qi = IDX_HEADS * IDX_DIM
    sc = 2 * LANES
    nsc = seq // sc
    gt = GROUP * tq
    jj, ii = np.meshgrid(np.arange(tq), np.arange(tq), indexing="ij")
    buckets = jnp.asarray(np.stack([_t5_bucket_np(tq + ii - jj), _t5_bucket_np(ii - jj)]))
    once = pl.Buffered(1)
    in_specs = [
        pl.BlockSpec(memory_space=pltpu.SMEM),
        pl.BlockSpec((2, tq, tq), lambda bi, i: (0, 0, 0)),
        pl.BlockSpec((1, tq, o_q), lambda bi, i: (bi, i, 0)),
        pl.BlockSpec((1, seq, o_kv), lambda bi, i: (bi, 0, o_q // o_kv), pipeline_mode=once),
        pl.BlockSpec((1, seq, o_kv), lambda bi, i: (bi, 0, o_q // o_kv + 1), pipeline_mode=once),
        pl.BlockSpec((1, tq, o_qi), lambda bi, i: (bi, i, (o_q + 2 * o_kv) // o_qi)),
        pl.BlockSpec((1, seq, LANES), lambda bi, i: (bi, 0, (o_q + 2 * o_kv + o_qi) // LANES),
                     pipeline_mode=once),
        pl.BlockSpec((1, tq, LANES), lambda bi, i: (bi, i, (o_q + 2 * o_kv + o_qi) // LANES)),
    ]
    scratch = [
        pltpu.VMEM((KV_HEADS, nsc, sc, ATT_DIM), BF16),
        pltpu.VMEM((KV_HEADS, nsc, ATT_DIM, sc), BF16),
        pltpu.VMEM((nsc, sc, LANES), BF16),
        pltpu.VMEM((nsc, sc, LANES), BF16),
        pltpu.VMEM((KV_HEADS, gt, ATT_DIM), BF16),
        pltpu.VMEM((IDX_HEADS // 4, 2 * tq, LANES), BF16),
        pltpu.VMEM((nsc, sc, tq), I32),
        pltpu.VMEM((nsc, sc, tq), F32),
        pltpu.VMEM((KV_HEADS, 5 * LANES, gt), F32),
        pltpu.VMEM((KV_HEADS, nsc, sc, gt), F32),
        pltpu.VMEM((KV_HEADS, SUBLANES, gt), F32),
        pltpu.VMEM((KV_HEADS, ATT_DIM, gt), F32),
        pltpu.VMEM((1, tq), I32),
    ]
    return pl.pallas_call(
        functools.partial(_dsa_kernel, seq=seq, tq=tq, topk=topk),
        grid=(b, seq // tq),
        in_specs=in_specs,
        out_specs=pl.BlockSpec((1, tq, o_q), lambda bi, i: (bi, i, 0)),
        out_shape=jax.ShapeDtypeStruct((b, seq, o_q), BF16),
        scratch_shapes=scratch,
        compiler_params=_cparams(2, 56),
        name="dsa_attention",
    )(rel_bias, buckets, proj, proj, proj, proj, proj, proj)


def kernel(x, norm_mix, norm_ffn, e_w_in, e_ret_gn, e_conv_w, e_conv_b, e_gate_a_w, e_gate_a_b,
           e_gate_x_w, e_gate_x_b, e_lambda, e_w_out, o_w_in, o_w_out, rel_bias,
           ffn_w_gu, ffn_conv_w, ffn_conv_b, ffn_w_down, final_norm):
    b, s, d = x.shape
    m = b * s
    tm = 512
    x2 = x.reshape(m, d)

    e_in = e_w_in.shape[-1]
    proj = norm_matmul(x2, norm_mix[0], e_w_in[0].astype(BF16), tm=1024, tn=512)
    proj = proj.reshape(b, s, e_in)
    ret = retention(proj, e_ret_gn[0], seq=s)
    lru = rglru(proj, e_conv_w[0], e_conv_b[0], e_gate_a_w[0], e_gate_a_b[0], e_gate_x_w[0],
                e_gate_x_b[0], e_lambda[0], seq=s,
                x_off=4 * RET_HEADS, y_off=4 * RET_HEADS + LRU_BLOCKS)
    x2 = matmul_residual(x2, ret.reshape(m, RET_WIDTH), 0, lru.reshape(m, LRU_WIDTH), 0,
                         e_w_out[0].astype(BF16), tm=tm)
    x2 = conv_ffn(x2, norm_ffn[0], ffn_w_gu[0].astype(BF16), ffn_conv_w[0], ffn_conv_b[0],
                  ffn_w_down[0].astype(BF16), final_norm, seq=s, tm=tm, tf=512,
                  final_norm=False)

    o_in = o_w_in.shape[-1]
    n_pad = -o_in % 512
    w_in = jnp.pad(o_w_in[0].astype(BF16), ((0, 0), (0, n_pad)))
    proj = norm_matmul(x2, norm_mix[1], w_in, tm=1024, tn=512)
    attn = dsa_attention(proj.reshape(b, s, o_in + n_pad), rel_bias, seq=s)
    attn = attn.reshape(m, ATT_HEADS * ATT_DIM)
    x2 = matmul_residual(x2, attn, 0, attn, 1, o_w_out[0].astype(BF16), tm=tm)
    x2 = conv_ffn(x2, norm_ffn[1], ffn_w_gu[1].astype(BF16), ffn_conv_w[1], ffn_conv_b[1],
                  ffn_w_down[1].astype(BF16), final_norm, seq=s, tm=tm, tf=512,
                  final_norm=True)
    return x2.reshape(b, s, d)
```

```python
import functools
import math

import numpy as np
import jax
import jax.numpy as jnp
from jax import lax
from jax.experimental import pallas as pl
from jax.experimental.pallas import tpu as pltpu

F32 = jnp.float32
BF16 = jnp.bfloat16
I32 = jnp.int32
I16 = jnp.int16

EPS = 1e-6
NEG_INF = -1e30
ROPE_BASE = 10000.0

RET_HEADS = 8
RET_DIM = 128
RET_WIDTH = RET_HEADS * RET_DIM
RET_CHUNK = 128
LRU_BLOCKS = 8
LRU_BLOCK_DIM = 128
LRU_WIDTH = LRU_BLOCKS * LRU_BLOCK_DIM
LRU_CONV = 4
LRU_C = 8.0

ATT_HEADS = 16
ATT_DIM = 128
KV_HEADS = 4
GROUP = ATT_HEADS // KV_HEADS
IDX_HEADS = 16
IDX_DIM = 64
TOPK_MAX = 256
REL_BUCKETS = 32
REL_MAX_DIST = 128
FFN_CONV = 3

LANES = 128
SUBLANES = 8
MIB = 2 ** 20
HALF16 = 2 ** 15

PROJ_TM, PROJ_TN = 1024, 512
OUT_TM = 512
FFN_TM, FFN_TF = 1024, 512


def _cparams(n_axes, vmem_mib):
    return pltpu.CompilerParams(
        dimension_semantics=("arbitrary",) * n_axes,
        vmem_limit_bytes=int(vmem_mib * MIB))


def _dot(a, b):
    return jnp.dot(a, b, preferred_element_type=F32)


def _dot_nt(a, b):
    return lax.dot_general(a, b, (((1,), (1,)), ((), ())), preferred_element_type=F32)


def _dot_tn(a, b):
    return lax.dot_general(a, b, (((0,), (0,)), ((), ())), preferred_element_type=F32)


def _rms_rows(x, g):
    ms = jnp.mean(x * x, axis=-1, keepdims=True)
    return x * lax.rsqrt(ms + EPS) * g


def _rms_to_ref(x_ref, g_ref, h_ref, row_chunk):
    g = g_ref[...]

    def body(r, c):
        sl = pl.ds(pl.multiple_of(r * row_chunk, row_chunk), row_chunk)
        h_ref[sl, :] = _rms_rows(x_ref[sl, :], g).astype(h_ref.dtype)
        return c

    lax.fori_loop(0, x_ref.shape[0] // row_chunk, body, 0)


def _norm_matmul_kernel(x_ref, g_ref, w_ref, o_ref, *rest, row_chunk, side_col):
    h_ref = rest[-1]
    j = pl.program_id(1)

    @pl.when(j == 0)
    def _():
        _rms_to_ref(x_ref, g_ref, h_ref, row_chunk)

    res = _dot(h_ref[...], w_ref[...])
    o_ref[...] = res.astype(o_ref.dtype)
    if side_col is not None:
        side_ref = rest[0]
        tn = w_ref.shape[1]

        @pl.when(j == side_col // tn)
        def _():
            off = side_col % tn
            side_ref[...] = res[:, off:off + LANES]


def norm_matmul(x, g, w, *, tm, tn, out_dtype=F32, side_col=None, row_chunk=128):
    m, d = x.shape
    n = w.shape[1]
    vmem = (2 * tm * d * 4 + 2 * d * tn * 2 + 2 * tm * tn * 4 + tm * d * 2) / MIB + 8
    out_specs = pl.BlockSpec((tm, tn), lambda i, j: (i, j))
    out_shape = jax.ShapeDtypeStruct((m, n), out_dtype)
    if side_col is not None:
        out_specs = [out_specs, pl.BlockSpec((tm, LANES), lambda i, j: (i, 0))]
        out_shape = [out_shape, jax.ShapeDtypeStruct((m, LANES), F32)]
    return pl.pallas_call(
        functools.partial(_norm_matmul_kernel, row_chunk=row_chunk, side_col=side_col),
        grid=(m // tm, n // tn),
        in_specs=[
            pl.BlockSpec((tm, d), lambda i, j: (i, 0)),
            pl.BlockSpec((1, d), lambda i, j: (0, 0)),
            pl.BlockSpec((d, tn), lambda i, j: (0, j)),
        ],
        out_specs=out_specs,
        out_shape=out_shape,
        scratch_shapes=[pltpu.VMEM((tm, d), BF16)],
        compiler_params=_cparams(2, vmem),
        name="norm_matmul",
    )(x, g.reshape(1, d), w)


def _matmul_residual_kernel(x_ref, a1_ref, a2_ref, w1_ref, w2_ref, o_ref):
    o_ref[...] = x_ref[...] + _dot(a1_ref[...], w1_ref[...]) + _dot(a2_ref[...], w2_ref[...])


def matmul_residual(x, a1, c1, a2, c2, w, *, tm):
    m, n = x.shape
    k1 = w.shape[0] // 2
    vmem = (4 * tm * n * 4 + 4 * tm * k1 * 2 + 4 * k1 * n * 2) / MIB + 8
    return pl.pallas_call(
        _matmul_residual_kernel,
        grid=(m // tm,),
        in_specs=[
            pl.BlockSpec((tm, n), lambda i: (i, 0)),
            pl.BlockSpec((tm, k1), lambda i: (i, c1)),
            pl.BlockSpec((tm, k1), lambda i: (i, c2)),
            pl.BlockSpec((k1, n), lambda i: (0, 0)),
            pl.BlockSpec((k1, n), lambda i: (1, 0)),
        ],
        out_specs=pl.BlockSpec((tm, n), lambda i: (i, 0)),
        out_shape=jax.ShapeDtypeStruct((m, n), F32),
        compiler_params=_cparams(1, vmem),
        name="matmul_residual",
    )(x, a1, a2, w, w)


def _ffn_kernel(x_ref, g_ref, wg_ref, wu_ref, cw_ref, cb_ref, wd_ref, gf_ref, o_ref,
                h_ref, carry_ref, *, tm, seq, row_chunk, final_norm):
    i = pl.program_id(0)
    j = pl.program_id(1)
    nj = pl.num_programs(1)

    @pl.when(j == 0)
    def _():
        _rms_to_ref(x_ref, g_ref, h_ref, row_chunk)
        o_ref[...] = x_ref[...]

    h = h_ref[...]
    g = _dot(h, wg_ref[...])
    u = _dot(h, wu_ref[...])

    seq_start = (i * tm) % seq == 0
    prev = jnp.where(seq_start, 0.0, carry_ref[j])
    carry_ref[j] = g[tm - SUBLANES:, :]
    row = lax.broadcasted_iota(I32, (SUBLANES, g.shape[1]), 0)
    g1 = pltpu.roll(g, 1, 0)
    g2 = pltpu.roll(g, 2, 0)
    g1_top = jnp.where(row == 0, prev[7:8, :], g1[:SUBLANES])
    g2_top = jnp.where(row == 0, prev[6:7, :], jnp.where(row == 1, prev[7:8, :], g2[:SUBLANES]))
    g1 = jnp.concatenate([g1_top, g1[SUBLANES:]], axis=0)
    g2 = jnp.concatenate([g2_top, g2[SUBLANES:]], axis=0)
    cw = cw_ref[...]
    gc = cb_ref[...] + cw[0:1] * g2 + cw[1:2] * g1 + cw[2:3] * g
    act = (gc * jax.nn.sigmoid(gc) * u).astype(BF16)
    o_ref[...] += _dot(act, wd_ref[...])

    if final_norm:
        @pl.when(j == nj - 1)
        def _():
            _rms_to_ref(o_ref, gf_ref, o_ref, row_chunk)


def conv_ffn(x, g, w_gu, conv_w, conv_b, w_down, g_final, *, seq, tm, tf,
             final_norm, row_chunk=128):
    m, d = x.shape
    f = w_down.shape[0]
    nf = f // tf
    vmem = (4 * tm * d * 4 + tm * d * 2 + 4 * d * tf * 2 + 2 * tf * d * 2) / MIB + 8
    return pl.pallas_call(
        functools.partial(_ffn_kernel, tm=tm, seq=seq, row_chunk=row_chunk,
                          final_norm=final_norm),
        grid=(m // tm, nf),
        in_specs=[
            pl.BlockSpec((tm, d), lambda i, j: (i, 0)),
            pl.BlockSpec((1, d), lambda i, j: (0, 0)),
            pl.BlockSpec((d, tf), lambda i, j: (0, j)),
            pl.BlockSpec((d, tf), lambda i, j: (0, j + nf)),
            pl.BlockSpec((FFN_CONV, tf), lambda i, j: (0, j)),
            pl.BlockSpec((1, tf), lambda i, j: (0, j)),
            pl.BlockSpec((tf, d), lambda i, j: (j, 0)),
            pl.BlockSpec((1, d), lambda i, j: (0, 0)),
        ],
        out_specs=pl.BlockSpec((tm, d), lambda i, j: (i, 0)),
        out_shape=jax.ShapeDtypeStruct((m, d), F32),
        scratch_shapes=[pltpu.VMEM((tm, d), BF16),
                        pltpu.VMEM((nf, SUBLANES, tf), F32)],
        compiler_params=_cparams(2, vmem),
        name="conv_ffn",
    )(x, g.reshape(1, d), w_gu, w_gu, conv_w, conv_b.reshape(1, f), w_down,
      g_final.reshape(1, d))


def _retention_kernel(q_ref, k_ref, v_ref, gate_ref, cos_ref, sin_ref, dec_ref, qd_ref,
                      kd_ref, cd_ref, gn_ref, o_ref, state_ref, *, seq, hp):
    c = RET_CHUNK
    scale = RET_DIM ** -0.5
    state_ref[...] = jnp.zeros(state_ref.shape, F32)

    def chunk(n, carry):
        sl = pl.ds(pl.multiple_of(n * c, c), c)
        cos = cos_ref[sl, :]
        sin = sin_ref[sl, :]
        for hh in range(hp):
            cols = slice(hh * RET_DIM, (hh + 1) * RET_DIM)
            q = q_ref[0, sl, cols]
            k = k_ref[0, sl, cols]
            v = v_ref[0, sl, cols].astype(BF16)
            state = state_ref[hh]
            qr = q * cos + pltpu.roll(q, RET_DIM // 2, 1) * sin
            kr = (k * cos + pltpu.roll(k, RET_DIM // 2, 1) * sin) * scale
            scores = _dot_nt(qr.astype(BF16), kr.astype(BF16)) * dec_ref[hh]
            y = _dot(scores.astype(BF16), v)
            y = y + _dot((qr * qd_ref[hh]).astype(BF16), state.astype(BF16))
            kv = _dot_tn((kr * kd_ref[hh]).astype(BF16), v)
            state_ref[hh] = cd_ref[hh][0:1, :] * state + kv
            y = y * lax.rsqrt(jnp.mean(y * y, axis=-1, keepdims=True) + EPS)
            y = y * gn_ref[:, cols]
            gate = gate_ref[0, sl, cols]
            o_ref[0, sl, cols] = (y * (gate * jax.nn.sigmoid(gate))).astype(o_ref.dtype)
        return carry

    lax.fori_loop(0, seq // c, chunk, 0)


def retention(proj, gn, *, seq, hp=4):
    b = proj.shape[0]
    h_ = RET_HEADS
    c = RET_CHUNK
    half = RET_DIM // 2
    freqs = ROPE_BASE ** (-jnp.arange(half, dtype=F32) / half)
    ang = jnp.arange(seq, dtype=F32)[:, None] * freqs[None, :]
    cos2 = jnp.concatenate([jnp.cos(ang), jnp.cos(ang)], axis=-1)
    sin2 = jnp.concatenate([-jnp.sin(ang), jnp.sin(ang)], axis=-1)
    log_g = jnp.log1p(-jnp.exp2(-5.0 - jnp.arange(h_, dtype=F32)))
    pos = jnp.arange(c, dtype=F32)
    diff = pos[:, None] - pos[None, :]
    inner = jnp.where(diff[None] >= 0,
                      jnp.exp(jnp.maximum(diff, 0.0)[None] * log_g[:, None, None]), 0.0)
    q_decay = jnp.exp((pos[None, :] + 1.0) * log_g[:, None])
    k_decay = jnp.exp((c - 1.0 - pos[None, :]) * log_g[:, None])
    chunk_decay = jnp.exp(c * log_g)
    qd = jnp.broadcast_to(q_decay[:, :, None], (h_, c, LANES))
    kd = jnp.broadcast_to(k_decay[:, :, None], (h_, c, LANES))
    cd = jnp.broadcast_to(chunk_decay[:, None, None], (h_, SUBLANES, LANES))

    ng = h_ // hp
    wide = hp * RET_DIM
    head_spec = lambda sec: pl.BlockSpec((1, seq, wide), lambda bi, hi: (bi, 0, sec * ng + hi))
    const_spec = pl.BlockSpec((seq, LANES), lambda bi, hi: (0, 0))
    per_head = lambda r: pl.BlockSpec((hp, r, LANES), lambda bi, hi: (hi, 0, 0))
    vmem = (2 * 4 * seq * wide * 4 + 2 * seq * wide * 2 + 4 * seq * LANES * 4) / MIB + 6
    return pl.pallas_call(
        functools.partial(_retention_kernel, seq=seq, hp=hp),
        grid=(b, ng),
        in_specs=[head_spec(0), head_spec(1), head_spec(2), head_spec(3),
                  const_spec, const_spec, per_head(c), per_head(c), per_head(c),
                  per_head(SUBLANES),
                  pl.BlockSpec((1, wide), lambda bi, hi: (0, hi))],
        out_specs=pl.BlockSpec((1, seq, wide), lambda bi, hi: (bi, 0, hi)),
        out_shape=jax.ShapeDtypeStruct((b, seq, RET_WIDTH), BF16),
        scratch_shapes=[pltpu.VMEM((hp, RET_DIM, RET_DIM), F32)],
        compiler_params=_cparams(2, vmem),
        name="retention",
    )(proj, proj, proj, proj, cos2, sin2, inner, qd, kd, cd, gn.reshape(1, RET_WIDTH))


def _shift_rows(v, d, row8):
    r = pltpu.roll(v, d, 0)
    top = jnp.where(row8 >= d, r[:SUBLANES], 0.0)
    return jnp.concatenate([top, r[SUBLANES:]], axis=0)


def _rglru_kernel(x_ref, y_ref, cw_ref, cb_ref, wa_ref, wx_ref, ba_ref, bx_ref, lam_ref,
                  o_ref, ga_ref, gb_ref, carry_ref, *, seq):
    x = x_ref[0]
    row = lax.broadcasted_iota(I32, x.shape, 0)
    cw = cw_ref[...]
    row8 = row[:SUBLANES]
    xc = cb_ref[...] + cw[0:1] * _shift_rows(x, 3, row8)
    xc = xc + cw[1:2] * _shift_rows(x, 2, row8)
    xc = xc + cw[2:3] * _shift_rows(x, 1, row8)
    xc = xc + cw[3:4] * x
    xb = xc.astype(BF16)
    r = jax.nn.sigmoid(_dot(xb, wa_ref[0].astype(BF16)) + ba_ref[...])
    ig = jax.nn.sigmoid(_dot(xb, wx_ref[0].astype(BF16)) + bx_ref[...])
    z = -lam_ref[...]
    softplus = jnp.maximum(z, 0.0) + jnp.log1p(jnp.exp(-jnp.abs(z)))
    log_a = -LRU_C * r * softplus
    a = jnp.exp(log_a)
    v = jnp.maximum(1.0 - a * a, 0.0)
    mult = jnp.where(v > 0.0, v * lax.rsqrt(v), 0.0)
    bb = mult * (ig * xc)

    def doubling(a, bb, idx, limit):
        d = 1
        while d < limit:
            ok = idx >= d
            a_sh = jnp.where(ok, pltpu.roll(a, d, 0), 1.0)
            b_sh = jnp.where(ok, pltpu.roll(bb, d, 0), 0.0)
            bb = bb + a * b_sh
            a = a * a_sh
            d *= 2
        return a, bb

    a, bb = doubling(a, bb, row & (SUBLANES - 1), SUBLANES)
    ga_ref[...] = a
    gb_ref[...] = bb
    ng = seq // SUBLANES
    last = pl.ds(SUBLANES - 1, ng, stride=SUBLANES)
    grow = lax.broadcasted_iota(I32, (ng, x.shape[1]), 0)
    _, h_end = doubling(ga_ref[last, :], gb_ref[last, :], grow, ng)
    carry_ref[...] = jnp.where(grow >= 1, pltpu.roll(h_end, 1, 0), 0.0)

    yv = y_ref[0]
    gelu = 0.5 * yv * (1.0 + jnp.tanh(math.sqrt(2.0 / math.pi) * (yv + 0.044715 * (yv * yv * yv))))
    h = jnp.concatenate(
        [bb[g * SUBLANES:(g + 1) * SUBLANES] + a[g * SUBLANES:(g + 1) * SUBLANES] * carry_ref[g:g + 1, :]
         for g in range(ng)], axis=0)
    o_ref[0] = (h * gelu).astype(o_ref.dtype)


def rglru(proj, conv_w, conv_b, wa, ba, wx, bx, lam, *, seq, x_off, y_off):
    b = proj.shape[0]
    nb = LRU_BLOCKS
    w = LRU_WIDTH
    blk = lambda off: pl.BlockSpec((1, seq, LANES), lambda bi, ni: (bi, 0, off + ni))
    vec = pl.BlockSpec((1, LANES), lambda bi, ni: (0, ni))
    mat = pl.BlockSpec((1, LRU_BLOCK_DIM, LRU_BLOCK_DIM), lambda bi, ni: (ni, 0, 0))
    return pl.pallas_call(
        functools.partial(_rglru_kernel, seq=seq),
        grid=(b, nb),
        in_specs=[blk(x_off), blk(y_off),
                  pl.BlockSpec((LRU_CONV, LANES), lambda bi, ni: (0, ni)),
                  vec, mat, mat, vec, vec, vec],
        out_specs=pl.BlockSpec((1, seq, LANES), lambda bi, ni: (bi, 0, ni)),
        out_shape=jax.ShapeDtypeStruct((b, seq, w), BF16),
        scratch_shapes=[pltpu.VMEM((seq, LANES), F32), pltpu.VMEM((seq, LANES), F32),
                        pltpu.VMEM((seq // SUBLANES, LANES), F32)],
        compiler_params=_cparams(2, 48),
        name="rglru",
    )(proj, proj, conv_w, conv_b.reshape(1, w), wa, wx, ba.reshape(1, w), bx.reshape(1, w),
      lam.reshape(1, w))


def _t5_bucket_np(rel):
    n = np.maximum(rel, 0)
    max_exact = REL_BUCKETS // 2
    nf = np.maximum(n, max_exact).astype(np.float64)
    large = max_exact + (np.log(nf / max_exact) / math.log(REL_MAX_DIST / max_exact)
                         * (REL_BUCKETS - max_exact)).astype(np.int32)
    large = np.minimum(large, REL_BUCKETS - 1)
    return np.where(n < max_exact, n, large).astype(np.int32)


def _dsa_kernel(rb_ref, bk_ref, q_ref, k_ref, v_ref, qi_ref, kwa_ref, kwq_ref, o_ref,
                kb_ref, vt_ref, kia_ref, kib_ref, qs_ref, qst_ref, key_ref, hi_ref, lo_ref,
                negm_ref, bias_ref, s_ref, m_ref, acc_ref, pos_ref, *, seq, tq, topk):
    bi = pl.program_id(0)
    i = pl.program_id(1)
    t0 = i * tq
    sc = 2 * LANES
    nsc = seq // sc
    gt = GROUP * tq
    nskip = topk // tq
    log2e = math.log2(math.e)
    att_scale = ATT_DIM ** -0.5 * log2e
    idx_scale = IDX_HEADS ** -0.5 * IDX_DIM ** -0.5
    nsel = (t0 + tq + sc - 1) // sc

    @pl.when(i == 0)
    def _():
        lane = lax.broadcasted_iota(I32, (sc, LANES), 1)
        for u in range(nsc):
            kblk = k_ref[0, u * sc:(u + 1) * sc, :]
            vblk = v_ref[0, u * sc:(u + 1) * sc, :]
            for kh in range(KV_HEADS):
                kb_ref[kh, u] = kblk[:, kh * ATT_DIM:(kh + 1) * ATT_DIM].astype(BF16)
                vt_ref[kh, u] = vblk[:, kh * ATT_DIM:(kh + 1) * ATT_DIM].astype(F32).T.astype(BF16)
            kw = kwa_ref[0, u * sc:(u + 1) * sc, :].astype(F32)
            ka = jnp.where(lane < IDX_DIM, kw, 0.0)
            kia_ref[u] = ka.astype(BF16)
            kib_ref[u] = pltpu.roll(ka, IDX_DIM, 1).astype(BF16)

    @pl.when((bi == 0) & (i == 0))
    def _():
        bias_ref[...] = jnp.zeros(bias_ref.shape, F32)

        def fill(bu, c):
            for r in range(2):
                hit = bk_ref[r] == bu
                for h in range(ATT_HEADS):
                    kh, g = divmod(h, GROUP)
                    val = (rb_ref[bu, h] - rb_ref[REL_BUCKETS - 1, h]) * log2e
                    rows = slice((2 + r) * LANES, (3 + r) * LANES)
                    cur = bias_ref[kh, rows, g * tq:(g + 1) * tq]
                    bias_ref[kh, rows, g * tq:(g + 1) * tq] = jnp.where(hit, val, cur)
            return c

        lax.fori_loop(0, REL_BUCKETS, fill, 0)

    for h in range(ATT_HEADS):
        kh, g = divmod(h, GROUP)
        qs_ref[kh, g * tq:(g + 1) * tq, :] = q_ref[0, :, h * ATT_DIM:(h + 1) * ATT_DIM].astype(BF16)

    row_s = lax.broadcasted_iota(I32, (sc, tq), 0)
    tidx = t0 + lax.broadcasted_iota(I32, (sc, tq), 1)

    @pl.when(i < nskip)
    def _():
        for u in range((nskip * tq + sc - 1) // sc):
            negm_ref[u] = jnp.where(u * sc + row_s <= tidx, 0.0, NEG_INF)

    @pl.when(i >= nskip)
    def _():
        for pp in range(IDX_HEADS // 4):
            for r in range(2):
                cols = slice((2 * pp + r) * LANES, (2 * pp + r + 1) * LANES)
                qst_ref[pp, r * tq:(r + 1) * tq, :] = qi_ref[0, :, cols].astype(BF16)
        wt = kwq_ref[0].T * idx_scale

        def head_w(h):
            return wt[IDX_DIM + h:IDX_DIM + h + 1, :]

        def score_unit(u, carry):
            ka = kia_ref[u]
            kb_ = kib_ref[u]
            acc = jnp.zeros((sc, tq), F32)
            for pp in range(IDX_HEADS // 4):
                q2 = qst_ref[pp]
                sa = jnp.maximum(_dot_nt(ka, q2), 0.0)
                sb = jnp.maximum(_dot_nt(kb_, q2), 0.0)
                acc = acc + sa[:, :tq] * head_w(4 * pp) + sb[:, :tq] * head_w(4 * pp + 1)
                acc = acc + sa[:, tq:] * head_w(4 * pp + 2) + sb[:, tq:] * head_w(4 * pp + 3)
            score = jnp.where(u * sc + row_s <= tidx, acc + 0.0, NEG_INF)
            bits = lax.bitcast_convert_type(score, I32)
            key = jnp.where(bits < 0, bits ^ jnp.int32(0x7FFFFFFF), bits)
            key_ref[u] = key
            hi_ref[u] = lax.shift_right_arithmetic(key, 16).astype(I16)
            lo_ref[u] = ((key & 0xFFFF) - HALF16).astype(I16)
            return carry

        lax.fori_loop(0, nsel, score_unit, 0)

        @pl.when(nsel % 2 == 1)
        def _():
            hi_ref[nsel] = jnp.full((sc, tq), -HALF16, I16)
            lo_ref[nsel] = jnp.full((sc, tq), -HALF16, I16)

        npair = (nsel + 1) // 2
        rows16 = 2 * SUBLANES
        one16, zero16 = jnp.int16(1), jnp.int16(0)

        def count16(ref, cand):
            c16 = jnp.broadcast_to(cand.astype(I16), (rows16, tq))

            def body(pi, accs):
                accs = list(accs)
                for w in range(2):
                    kk = ref[2 * pi + w]
                    for r in range(sc // rows16):
                        hit = jnp.where(kk[r * rows16:(r + 1) * rows16] >= c16, one16, zero16)
                        accs[r % len(accs)] = accs[r % len(accs)] + hit
                return tuple(accs)

            accs = lax.fori_loop(0, npair, body,
                                 tuple(jnp.zeros((rows16, tq), I16) for _ in range(4)))
            tot = (accs[0] + accs[1]) + (accs[2] + accs[3])
            return tot.astype(I32).sum(axis=0, keepdims=True)

        def search16(ref):
            def step(it, thr):
                cand = thr + lax.shift_left(jnp.int32(1), 15 - it)
                return jnp.where(count16(ref, cand) >= topk, cand, thr)
            return lax.fori_loop(0, 16, step, jnp.full((1, tq), -HALF16, I32))

        t_hi = search16(hi_ref)
        t_hi16 = jnp.broadcast_to(t_hi.astype(I16), (rows16, tq))

        def fold(pi, carry):
            for w in range(2):
                u = 2 * pi + w
                for r in range(sc // rows16):
                    rs = slice(r * rows16, (r + 1) * rows16)
                    hi = hi_ref[u, rs, :]
                    lo_ref[u, rs, :] = jnp.where(
                        hi > t_hi16, jnp.int16(HALF16 - 1),
                        jnp.where(hi == t_hi16, lo_ref[u, rs, :], jnp.int16(-HALF16)))
            return carry

        lax.fori_loop(0, npair, fold, 0)
        t_lo = search16(lo_ref)
        thr = t_hi * (2 * HALF16) + (t_lo + HALF16)

        def count(pred):
            def body(c2, acc8):
                hit = pred(key_ref[c2], c2 * sc + row_s).astype(I32)
                return acc8 + hit.reshape(sc // SUBLANES, SUBLANES, tq).sum(axis=0)
            acc8 = lax.fori_loop(0, nsel, body, jnp.zeros((SUBLANES, tq), I32))
            return acc8.sum(axis=0, keepdims=True)

        need = topk - count(lambda kk, sidx: kk > thr)
        n_tie = count(lambda kk, sidx: kk == thr)

        nbits = int(seq).bit_length()
        pos_ref[...] = jnp.full((1, tq), 2 ** nbits, I32)

        @pl.when(jnp.max(jnp.where(n_tie != need, 1, 0)) > 0)
        def _():
            def tie_step(it, pos):
                cand = pos + lax.shift_left(jnp.int32(1), nbits - 1 - it)
                cnt = count(lambda kk, sidx: (kk == thr) & (sidx < cand))
                return jnp.where(cnt <= need, cand, pos)

            pos_ref[...] = lax.fori_loop(0, nbits, tie_step, jnp.zeros((1, tq), I32))

        pos = pos_ref[...]

        def mask_unit(u, carry):
            kk = key_ref[u]
            sidx = u * sc + row_s
            sel = (kk > thr) | ((kk == thr) & (sidx < pos))
            negm_ref[u] = jnp.where(sel & (sidx <= tidx), 0.0, NEG_INF)
            return carry

        lax.fori_loop(0, nsel, mask_unit, 0)

    nfar = jnp.maximum((i - 1) // 2, 0)
    part = (sc // SUBLANES, SUBLANES, gt)

    def logits_unit(u, mparts, with_bias):
        nm = negm_ref[u]
        nm = jnp.concatenate([nm] * GROUP, axis=1)
        out = []
        for kh in range(KV_HEADS):
            s = _dot_nt(kb_ref[kh, u], qs_ref[kh]) * att_scale + nm
            if with_bias:
                start = pl.multiple_of((2 * u - i + 3) * LANES, LANES)
                s = s + bias_ref[kh, pl.ds(start, sc), :]
            s_ref[kh, u] = s
            out.append(jnp.maximum(mparts[kh], s.reshape(part).max(axis=0)))
        return tuple(out)

    mparts = tuple(jnp.full((SUBLANES, gt), -jnp.inf, F32) for _ in range(KV_HEADS))
    mparts = lax.fori_loop(0, nfar, lambda u, c: logits_unit(u, c, False), mparts)
    mparts = lax.fori_loop(nfar, nsel, lambda u, c: logits_unit(u, c, True), mparts)
    for kh in range(KV_HEADS):
        m_ref[kh] = jnp.broadcast_to(mparts[kh].max(axis=0, keepdims=True), (SUBLANES, gt))
    acc_ref[...] = jnp.zeros(acc_ref.shape, F32)

    def probs_unit(u, lparts):
        out = []
        for kh in range(KV_HEADS):
            p = jnp.exp2(s_ref[kh, u].reshape(part) - m_ref[kh][None])
            out.append(lparts[kh] + p.sum(axis=0))
            acc_ref[kh] += _dot(vt_ref[kh, u], p.reshape(sc, gt).astype(BF16))
        return tuple(out)

    lparts = tuple(jnp.zeros((SUBLANES, gt), F32) for _ in range(KV_HEADS))
    lparts = lax.fori_loop(0, nsel, probs_unit, lparts)
    for kh in range(KV_HEADS):
        out_t = acc_ref[kh] / lparts[kh].sum(axis=0, keepdims=True)
        for g in range(GROUP):
            h = kh * GROUP + g
            o_ref[0, :, h * ATT_DIM:(h + 1) * ATT_DIM] = (
                out_t[:, g * tq:(g + 1) * tq].T.astype(o_ref.dtype))


def dsa_attention(proj, kw_f32, rel_bias, *, seq, tq=LANES):
    assert tq == LANES
    b = proj.shape[0]
    topk = min(TOPK_MAX, seq // 4)
    assert topk % tq == 0 and seq % (2 * LANES) == 0
    o_q = ATT_HEADS * ATT_DIM
    o_kv = KV_HEADS * ATT_DIM
    o_qi = IDX_HEADS * IDX_DIM
    sc = 2 * LANES
    nsc = seq // sc
    gt = GROUP * tq
    jj, ii = np.meshgrid(np.arange(tq), np.arange(tq), indexing="ij")
    buckets = jnp.asarray(np.stack([_t5_bucket_np(tq + ii - jj), _t5_bucket_np(ii - jj)]))
    once = pl.Buffered(1)
    in_specs = [
        pl.BlockSpec(memory_space=pltpu.SMEM),
        pl.BlockSpec((2, tq, tq), lambda bi, i: (0, 0, 0)),
        pl.BlockSpec((1, tq, o_q), lambda bi, i: (bi, i, 0)),
        pl.BlockSpec((1, seq, o_kv), lambda bi, i: (bi, 0, o_q // o_kv), pipeline_mode=once),
        pl.BlockSpec((1, seq, o_kv), lambda bi, i: (bi, 0, o_q // o_kv + 1), pipeline_mode=once),
        pl.BlockSpec((1, tq, o_qi), lambda bi, i: (bi, i, (o_q + 2 * o_kv) // o_qi)),
        pl.BlockSpec((1, seq, LANES), lambda bi, i: (bi, 0, (o_q + 2 * o_kv + o_qi) // LANES),
                     pipeline_mode=once),
        pl.BlockSpec((1, tq, LANES), lambda bi, i: (bi, i, 0)),
    ]
    scratch = [
        pltpu.VMEM((KV_HEADS, nsc, sc, ATT_DIM), BF16),
        pltpu.VMEM((KV_HEADS, nsc, ATT_DIM, sc), BF16),
        pltpu.VMEM((nsc, sc, LANES), BF16),
        pltpu.VMEM((nsc, sc, LANES), BF16),
        pltpu.VMEM((KV_HEADS, gt, ATT_DIM), BF16),
        pltpu.VMEM((IDX_HEADS // 4, 2 * tq, LANES), BF16),
        pltpu.VMEM((nsc, sc, tq), I32),
        pltpu.VMEM((nsc, sc, tq), I16),
        pltpu.VMEM((nsc, sc, tq), I16),
        pltpu.VMEM((nsc, sc, tq), F32),
        pltpu.VMEM((KV_HEADS, 5 * LANES, gt), F32),
        pltpu.VMEM((KV_HEADS, nsc, sc, gt), F32),
        pltpu.VMEM((KV_HEADS, SUBLANES, gt), F32),
        pltpu.VMEM((KV_HEADS, ATT_DIM, gt), F32),
        pltpu.VMEM((1, tq), I32),
    ]
    return pl.pallas_call(
        functools.partial(_dsa_kernel, seq=seq, tq=tq, topk=topk),
        grid=(b, seq // tq),
        in_specs=in_specs,
        out_specs=pl.BlockSpec((1, tq, o_q), lambda bi, i: (bi, i, 0)),
        out_shape=jax.ShapeDtypeStruct((b, seq, o_q), BF16),
        scratch_shapes=scratch,
        compiler_params=_cparams(2, 56),
        name="dsa_attention",
    )(rel_bias, buckets, proj, proj, proj, proj, proj, kw_f32)


def kernel(x, norm_mix, norm_ffn, e_w_in, e_ret_gn, e_conv_w, e_conv_b, e_gate_a_w, e_gate_a_b,
           e_gate_x_w, e_gate_x_b, e_lambda, e_w_out, o_w_in, o_w_out, rel_bias,
           ffn_w_gu, ffn_conv_w, ffn_conv_b, ffn_w_down, final_norm):
    b, s, d = x.shape
    m = b * s
    x2 = x.reshape(m, d)

    e_in = e_w_in.shape[-1]
    proj = norm_matmul(x2, norm_mix[0], e_w_in[0].astype(BF16), tm=PROJ_TM, tn=PROJ_TN)
    proj = proj.reshape(b, s, e_in)
    ret = retention(proj, e_ret_gn[0], seq=s)
    lru = rglru(proj, e_conv_w[0], e_conv_b[0], e_gate_a_w[0], e_gate_a_b[0], e_gate_x_w[0],
                e_gate_x_b[0], e_lambda[0], seq=s,
                x_off=4 * RET_HEADS, y_off=4 * RET_HEADS + LRU_BLOCKS)
    x2 = matmul_residual(x2, ret.reshape(m, RET_WIDTH), 0, lru.reshape(m, LRU_WIDTH), 0,
                         e_w_out[0].astype(BF16), tm=OUT_TM)
    x2 = conv_ffn(x2, norm_ffn[0], ffn_w_gu[0].astype(BF16), ffn_conv_w[0], ffn_conv_b[0],
                  ffn_w_down[0].astype(BF16), final_norm, seq=s, tm=FFN_TM, tf=FFN_TF,
                  final_norm=False)

    o_in = o_w_in.shape[-1]
    n_pad = -o_in % PROJ_TN
    w_in = jnp.pad(o_w_in[0].astype(BF16), ((0, 0), (0, n_pad)))
    kw_col = ATT_HEADS * ATT_DIM + 2 * KV_HEADS * ATT_DIM + IDX_HEADS * IDX_DIM
    proj, kw_f32 = norm_matmul(x2, norm_mix[1], w_in, tm=PROJ_TM, tn=PROJ_TN, out_dtype=BF16,
                               side_col=kw_col)
    attn = dsa_attention(proj.reshape(b, s, o_in + n_pad), kw_f32.reshape(b, s, LANES),
                         rel_bias, seq=s)
    attn = attn.reshape(m, ATT_HEADS * ATT_DIM)
    x2 = matmul_residual(x2, attn, 0, attn, 1, o_w_out[0].astype(BF16), tm=OUT_TM)
    x2 = conv_ffn(x2, norm_ffn[1], ffn_w_gu[1].astype(BF16), ffn_conv_w[1], ffn_conv_b[1],
                  ffn_w_down[1].astype(BF16), final_norm, seq=s, tm=FFN_TM, tf=FFN_TF,
                  final_norm=True)
    return x2.reshape(b, s, d)
```

```python
import functools
import math

import numpy as np
import jax
import jax.numpy as jnp
from jax import lax
from jax.experimental import pallas as pl
from jax.experimental.pallas import tpu as pltpu

F32 = jnp.float32
BF16 = jnp.bfloat16
I32 = jnp.int32
I16 = jnp.int16

EPS = 1e-6
NEG_INF = -1e30
ROPE_BASE = 10000.0

RET_HEADS = 8
RET_DIM = 128
RET_WIDTH = RET_HEADS * RET_DIM
RET_CHUNK = 128
LRU_BLOCKS = 8
LRU_BLOCK_DIM = 128
LRU_WIDTH = LRU_BLOCKS * LRU_BLOCK_DIM
LRU_CONV = 4
LRU_C = 8.0

ATT_HEADS = 16
ATT_DIM = 128
KV_HEADS = 4
GROUP = ATT_HEADS // KV_HEADS
IDX_HEADS = 16
IDX_DIM = 64
TOPK_MAX = 256
REL_BUCKETS = 32
REL_MAX_DIST = 128
FFN_CONV = 3

LANES = 128
SUBLANES = 8
MIB = 2 ** 20
HALF16 = 2 ** 15

PROJ_TM, PROJ_TN = 1024, 512
OUT_TM = 512
FFN_TM, FFN_TF = 1024, 512


def _cparams(n_axes, vmem_mib):
    return pltpu.CompilerParams(
        dimension_semantics=("arbitrary",) * n_axes,
        vmem_limit_bytes=int(vmem_mib * MIB))


def _dot(a, b):
    return jnp.dot(a, b, preferred_element_type=F32)


def _dot_nt(a, b):
    return lax.dot_general(a, b, (((1,), (1,)), ((), ())), preferred_element_type=F32)


def _dot_tn(a, b):
    return lax.dot_general(a, b, (((0,), (0,)), ((), ())), preferred_element_type=F32)


def _rms_rows(x, g):
    ms = jnp.mean(x * x, axis=-1, keepdims=True)
    return x * lax.rsqrt(ms + EPS) * g


def _rms_to_ref(x_ref, g_ref, h_ref, row_chunk):
    g = g_ref[...]

    def body(r, c):
        sl = pl.ds(pl.multiple_of(r * row_chunk, row_chunk), row_chunk)
        h_ref[sl, :] = _rms_rows(x_ref[sl, :], g).astype(h_ref.dtype)
        return c

    lax.fori_loop(0, x_ref.shape[0] // row_chunk, body, 0)


def _norm_matmul_kernel(x0_ref, xn_ref, g_ref, w_ref, o_ref, *rest, row_chunk, side_col):
    h_ref = rest[-1]
    i = pl.program_id(0)
    j = pl.program_id(1)
    n_chunks = xn_ref.shape[0] // row_chunk

    @pl.when((i == 0) & (j == 0))
    def _():
        _rms_to_ref(x0_ref, g_ref, h_ref.at[0], row_chunk)

    sl = pl.ds(pl.multiple_of(jnp.minimum(j, n_chunks - 1) * row_chunk, row_chunk), row_chunk)
    h_ref[(i + 1) % 2, sl, :] = _rms_rows(xn_ref[sl, :], g_ref[...]).astype(h_ref.dtype)

    res = _dot(h_ref[i % 2], w_ref[...])
    o_ref[...] = res.astype(o_ref.dtype)
    if side_col is not None:
        side_ref = rest[0]
        tn = w_ref.shape[1]

        @pl.when(j == side_col // tn)
        def _():
            off = side_col % tn
            side_ref[...] = res[:, off:off + LANES]


def norm_matmul(x, g, w, *, tm, tn, out_dtype=F32, side_col=None, row_chunk=128):
    m, d = x.shape
    n = w.shape[1]
    ni = m // tm
    assert n // tn >= tm // row_chunk
    vmem = (3 * tm * d * 4 + 2 * d * tn * 2 + 2 * tm * tn * 4 + 2 * tm * d * 2) / MIB + 8
    out_specs = pl.BlockSpec((tm, tn), lambda i, j: (i, j))
    out_shape = jax.ShapeDtypeStruct((m, n), out_dtype)
    if side_col is not None:
        out_specs = [out_specs, pl.BlockSpec((tm, LANES), lambda i, j: (i, 0))]
        out_shape = [out_shape, jax.ShapeDtypeStruct((m, LANES), F32)]
    return pl.pallas_call(
        functools.partial(_norm_matmul_kernel, row_chunk=row_chunk, side_col=side_col),
        grid=(ni, n // tn),
        in_specs=[
            pl.BlockSpec((tm, d), lambda i, j: (0, 0), pipeline_mode=pl.Buffered(1)),
            pl.BlockSpec((tm, d), lambda i, j: (jnp.minimum(i + 1, ni - 1), 0)),
            pl.BlockSpec((1, d), lambda i, j: (0, 0)),
            pl.BlockSpec((d, tn), lambda i, j: (0, j)),
        ],
        out_specs=out_specs,
        out_shape=out_shape,
        scratch_shapes=[pltpu.VMEM((2, tm, d), BF16)],
        compiler_params=_cparams(2, vmem),
        name="norm_matmul",
    )(x, x, g.reshape(1, d), w)


def _matmul_residual_kernel(x_ref, a1_ref, a2_ref, w1_ref, w2_ref, o_ref):
    o_ref[...] = x_ref[...] + _dot(a1_ref[...], w1_ref[...]) + _dot(a2_ref[...], w2_ref[...])


def matmul_residual(x, a1, c1, a2, c2, w, *, tm):
    m, n = x.shape
    k1 = w.shape[0] // 2
    vmem = (4 * tm * n * 4 + 4 * tm * k1 * 2 + 4 * k1 * n * 2) / MIB + 8
    return pl.pallas_call(
        _matmul_residual_kernel,
        grid=(m // tm,),
        in_specs=[
            pl.BlockSpec((tm, n), lambda i: (i, 0)),
            pl.BlockSpec((tm, k1), lambda i: (i, c1)),
            pl.BlockSpec((tm, k1), lambda i: (i, c2)),
            pl.BlockSpec((k1, n), lambda i: (0, 0)),
            pl.BlockSpec((k1, n), lambda i: (1, 0)),
        ],
        out_specs=pl.BlockSpec((tm, n), lambda i: (i, 0)),
        out_shape=jax.ShapeDtypeStruct((m, n), F32),
        compiler_params=_cparams(1, vmem),
        name="matmul_residual",
    )(x, a1, a2, w, w)


def _ffn_kernel(x_ref, g_ref, wg_ref, wu_ref, cw_ref, cb_ref, wd_ref, gf_ref, o_ref,
                h_ref, carry_ref, *, tm, seq, row_chunk, final_norm):
    i = pl.program_id(0)
    j = pl.program_id(1)
    nj = pl.num_programs(1)

    @pl.when(j == 0)
    def _():
        _rms_to_ref(x_ref, g_ref, h_ref, row_chunk)
        o_ref[...] = x_ref[...]

    h = h_ref[...]
    g = _dot(h, wg_ref[...])
    u = _dot(h, wu_ref[...])

    seq_start = (i * tm) % seq == 0
    prev = jnp.where(seq_start, 0.0, carry_ref[j])
    carry_ref[j] = g[tm - SUBLANES:, :]
    row = lax.broadcasted_iota(I32, (SUBLANES, g.shape[1]), 0)
    g1 = pltpu.roll(g, 1, 0)
    g2 = pltpu.roll(g, 2, 0)
    g1_top = jnp.where(row == 0, prev[7:8, :], g1[:SUBLANES])
    g2_top = jnp.where(row == 0, prev[6:7, :], jnp.where(row == 1, prev[7:8, :], g2[:SUBLANES]))
    g1 = jnp.concatenate([g1_top, g1[SUBLANES:]], axis=0)
    g2 = jnp.concatenate([g2_top, g2[SUBLANES:]], axis=0)
    cw = cw_ref[...]
    gc = cb_ref[...] + cw[0:1] * g2 + cw[1:2] * g1 + cw[2:3] * g
    act = (gc * jax.nn.sigmoid(gc) * u).astype(BF16)
    o_ref[...] += _dot(act, wd_ref[...])

    if final_norm:
        @pl.when(j == nj - 1)
        def _():
            _rms_to_ref(o_ref, gf_ref, o_ref, row_chunk)


def conv_ffn(x, g, w_gu, conv_w, conv_b, w_down, g_final, *, seq, tm, tf,
             final_norm, row_chunk=128):
    m, d = x.shape
    f = w_down.shape[0]
    nf = f // tf
    vmem = (4 * tm * d * 4 + tm * d * 2 + 4 * d * tf * 2 + 2 * tf * d * 2) / MIB + 8
    return pl.pallas_call(
        functools.partial(_ffn_kernel, tm=tm, seq=seq, row_chunk=row_chunk,
                          final_norm=final_norm),
        grid=(m // tm, nf),
        in_specs=[
            pl.BlockSpec((tm, d), lambda i, j: (i, 0)),
            pl.BlockSpec((1, d), lambda i, j: (0, 0)),
            pl.BlockSpec((d, tf), lambda i, j: (0, j)),
            pl.BlockSpec((d, tf), lambda i, j: (0, j + nf)),
            pl.BlockSpec((FFN_CONV, tf), lambda i, j: (0, j)),
            pl.BlockSpec((1, tf), lambda i, j: (0, j)),
            pl.BlockSpec((tf, d), lambda i, j: (j, 0)),
            pl.BlockSpec((1, d), lambda i, j: (0, 0)),
        ],
        out_specs=pl.BlockSpec((tm, d), lambda i, j: (i, 0)),
        out_shape=jax.ShapeDtypeStruct((m, d), F32),
        scratch_shapes=[pltpu.VMEM((tm, d), BF16),
                        pltpu.VMEM((nf, SUBLANES, tf), F32)],
        compiler_params=_cparams(2, vmem),
        name="conv_ffn",
    )(x, g.reshape(1, d), w_gu, w_gu, conv_w, conv_b.reshape(1, f), w_down,
      g_final.reshape(1, d))


def _retention_kernel(q_ref, k_ref, v_ref, gate_ref, cos_ref, sin_ref, dec_ref, qd_ref,
                      kd_ref, cd_ref, gn_ref, o_ref, state_ref, *, seq, hp):
    c = RET_CHUNK
    cpi = 4
    scale = RET_DIM ** -0.5
    state_ref[...] = jnp.zeros(state_ref.shape, F32)

    def chunk_group(gi, carry):
        sls = [pl.ds(pl.multiple_of((gi * cpi + cc) * c, c), c) for cc in range(cpi)]
        cos = [cos_ref[sl, :] for sl in sls]
        sin = [sin_ref[sl, :] for sl in sls]
        for hh in range(hp):
            cols = slice(hh * RET_DIM, (hh + 1) * RET_DIM)
            state = state_ref[hh]
            for cc, sl in enumerate(sls):
                q = q_ref[0, sl, cols]
                k = k_ref[0, sl, cols]
                v = v_ref[0, sl, cols].astype(BF16)
                qr = q * cos[cc] + pltpu.roll(q, RET_DIM // 2, 1) * sin[cc]
                kr = (k * cos[cc] + pltpu.roll(k, RET_DIM // 2, 1) * sin[cc]) * scale
                scores = _dot_nt(qr.astype(BF16), kr.astype(BF16)) * dec_ref[hh]
                y = _dot(scores.astype(BF16), v)
                y = y + _dot((qr * qd_ref[hh]).astype(BF16), state.astype(BF16))
                kv = _dot_tn((kr * kd_ref[hh]).astype(BF16), v)
                state = cd_ref[hh][0:1, :] * state + kv
                y = y * lax.rsqrt(jnp.mean(y * y, axis=-1, keepdims=True) + EPS)
                y = y * gn_ref[:, cols]
                gate = gate_ref[0, sl, cols]
                o_ref[0, sl, cols] = (y * (gate * jax.nn.sigmoid(gate))).astype(o_ref.dtype)
            state_ref[hh] = state
        return carry

    lax.fori_loop(0, seq // (c * cpi), chunk_group, 0)


def retention(proj, gn, *, seq, hp=4):
    b = proj.shape[0]
    h_ = RET_HEADS
    c = RET_CHUNK
    half = RET_DIM // 2
    freqs = ROPE_BASE ** (-jnp.arange(half, dtype=F32) / half)
    ang = jnp.arange(seq, dtype=F32)[:, None] * freqs[None, :]
    cos2 = jnp.concatenate([jnp.cos(ang), jnp.cos(ang)], axis=-1)
    sin2 = jnp.concatenate([-jnp.sin(ang), jnp.sin(ang)], axis=-1)
    log_g = jnp.log1p(-jnp.exp2(-5.0 - jnp.arange(h_, dtype=F32)))
    pos = jnp.arange(c, dtype=F32)
    diff = pos[:, None] - pos[None, :]
    inner = jnp.where(diff[None] >= 0,
                      jnp.exp(jnp.maximum(diff, 0.0)[None] * log_g[:, None, None]), 0.0)
    q_decay = jnp.exp((pos[None, :] + 1.0) * log_g[:, None])
    k_decay = jnp.exp((c - 1.0 - pos[None, :]) * log_g[:, None])
    chunk_decay = jnp.exp(c * log_g)
    qd = jnp.broadcast_to(q_decay[:, :, None], (h_, c, LANES))
    kd = jnp.broadcast_to(k_decay[:, :, None], (h_, c, LANES))
    cd = jnp.broadcast_to(chunk_decay[:, None, None], (h_, SUBLANES, LANES))

    ng = h_ // hp
    wide = hp * RET_DIM
    head_spec = lambda sec: pl.BlockSpec((1, seq, wide), lambda bi, hi: (bi, 0, sec * ng + hi))
    const_spec = pl.BlockSpec((seq, LANES), lambda bi, hi: (0, 0))
    per_head = lambda r: pl.BlockSpec((hp, r, LANES), lambda bi, hi: (hi, 0, 0))
    vmem = (2 * 4 * seq * wide * 4 + 2 * seq * wide * 2 + 4 * seq * LANES * 4) / MIB + 6
    return pl.pallas_call(
        functools.partial(_retention_kernel, seq=seq, hp=hp),
        grid=(b, ng),
        in_specs=[head_spec(0), head_spec(1), head_spec(2), head_spec(3),
                  const_spec, const_spec, per_head(c), per_head(c), per_head(c),
                  per_head(SUBLANES),
                  pl.BlockSpec((1, wide), lambda bi, hi: (0, hi))],
        out_specs=pl.BlockSpec((1, seq, wide), lambda bi, hi: (bi, 0, hi)),
        out_shape=jax.ShapeDtypeStruct((b, seq, RET_WIDTH), BF16),
        scratch_shapes=[pltpu.VMEM((hp, RET_DIM, RET_DIM), F32)],
        compiler_params=_cparams(2, vmem),
        name="retention",
    )(proj, proj, proj, proj, cos2, sin2, inner, qd, kd, cd, gn.reshape(1, RET_WIDTH))


def _shift_rows(v, d, row8):
    r = pltpu.roll(v, d, 0)
    top = jnp.where(row8 >= d, r[:SUBLANES], 0.0)
    return jnp.concatenate([top, r[SUBLANES:]], axis=0)


def _rglru_kernel(x_ref, y_ref, cw_ref, cb_ref, wa_ref, wx_ref, ba_ref, bx_ref, lam_ref,
                  o_ref, ga_ref, gb_ref, carry_ref, *, seq):
    x = x_ref[0]
    row = lax.broadcasted_iota(I32, x.shape, 0)
    cw = cw_ref[...]
    row8 = row[:SUBLANES]
    xc = cb_ref[...] + cw[0:1] * _shift_rows(x, 3, row8)
    xc = xc + cw[1:2] * _shift_rows(x, 2, row8)
    xc = xc + cw[2:3] * _shift_rows(x, 1, row8)
    xc = xc + cw[3:4] * x
    xb = xc.astype(BF16)
    r = jax.nn.sigmoid(_dot(xb, wa_ref[0].astype(BF16)) + ba_ref[...])
    ig = jax.nn.sigmoid(_dot(xb, wx_ref[0].astype(BF16)) + bx_ref[...])
    z = -lam_ref[...]
    softplus = jnp.maximum(z, 0.0) + jnp.log1p(jnp.exp(-jnp.abs(z)))
    log_a = -LRU_C * r * softplus
    a = jnp.exp(log_a)
    v = jnp.maximum(1.0 - a * a, 0.0)
    mult = jnp.where(v > 0.0, v * lax.rsqrt(v), 0.0)
    bb = mult * (ig * xc)

    def doubling(a, bb, idx, limit):
        d = 1
        while d < limit:
            ok = idx >= d
            a_sh = jnp.where(ok, pltpu.roll(a, d, 0), 1.0)
            b_sh = jnp.where(ok, pltpu.roll(bb, d, 0), 0.0)
            bb = bb + a * b_sh
            a = a * a_sh
            d *= 2
        return a, bb

    a, bb = doubling(a, bb, row & (SUBLANES - 1), SUBLANES)
    ga_ref[...] = a
    gb_ref[...] = bb
    ng = seq // SUBLANES
    last = pl.ds(SUBLANES - 1, ng, stride=SUBLANES)
    grow = lax.broadcasted_iota(I32, (ng, x.shape[1]), 0)
    _, h_end = doubling(ga_ref[last, :], gb_ref[last, :], grow, ng)
    carry_ref[...] = jnp.where(grow >= 1, pltpu.roll(h_end, 1, 0), 0.0)

    yv = y_ref[0]
    gelu = 0.5 * yv * (1.0 + jnp.tanh(math.sqrt(2.0 / math.pi) * (yv + 0.044715 * (yv * yv * yv))))
    h = jnp.concatenate(
        [bb[g * SUBLANES:(g + 1) * SUBLANES] + a[g * SUBLANES:(g + 1) * SUBLANES] * carry_ref[g:g + 1, :]
         for g in range(ng)], axis=0)
    o_ref[0] = (h * gelu).astype(o_ref.dtype)


def rglru(proj, conv_w, conv_b, wa, ba, wx, bx, lam, *, seq, x_off, y_off):
    b = proj.shape[0]
    nb = LRU_BLOCKS
    w = LRU_WIDTH
    blk = lambda off: pl.BlockSpec((1, seq, LANES), lambda bi, ni: (bi, 0, off + ni))
    vec = pl.BlockSpec((1, LANES), lambda bi, ni: (0, ni))
    mat = pl.BlockSpec((1, LRU_BLOCK_DIM, LRU_BLOCK_DIM), lambda bi, ni: (ni, 0, 0))
    return pl.pallas_call(
        functools.partial(_rglru_kernel, seq=seq),
        grid=(b, nb),
        in_specs=[blk(x_off), blk(y_off),
                  pl.BlockSpec((LRU_CONV, LANES), lambda bi, ni: (0, ni)),
                  vec, mat, mat, vec, vec, vec],
        out_specs=pl.BlockSpec((1, seq, LANES), lambda bi, ni: (bi, 0, ni)),
        out_shape=jax.ShapeDtypeStruct((b, seq, w), BF16),
        scratch_shapes=[pltpu.VMEM((seq, LANES), F32), pltpu.VMEM((seq, LANES), F32),
                        pltpu.VMEM((seq // SUBLANES, LANES), F32)],
        compiler_params=_cparams(2, 48),
        name="rglru",
    )(proj, proj, conv_w, conv_b.reshape(1, w), wa, wx, ba.reshape(1, w), bx.reshape(1, w),
      lam.reshape(1, w))


def _t5_bucket_np(rel):
    n = np.maximum(rel, 0)
    max_exact = REL_BUCKETS // 2
    nf = np.maximum(n, max_exact).astype(np.float64)
    large = max_exact + (np.log(nf / max_exact) / math.log(REL_MAX_DIST / max_exact)
                         * (REL_BUCKETS - max_exact)).astype(np.int32)
    large = np.minimum(large, REL_BUCKETS - 1)
    return np.where(n < max_exact, n, large).astype(np.int32)


def _dsa_kernel(rb_ref, bk_ref, q_ref, k_ref, v_ref, qi_ref, kwa_ref, kwq_ref, o_ref,
                kb_ref, vt_ref, kia_ref, kib_ref, qs_ref, qst_ref, key_ref, hi_ref, lo_ref,
                negm_ref, bias_ref, s_ref, m_ref, acc_ref, pos_ref, *, seq, tq, topk):
    bi = pl.program_id(0)
    i = pl.program_id(1)
    t0 = i * tq
    sc = 2 * LANES
    nsc = seq // sc
    gt = GROUP * tq
    nskip = topk // tq
    log2e = math.log2(math.e)
    att_scale = ATT_DIM ** -0.5 * log2e
    idx_scale = IDX_HEADS ** -0.5 * IDX_DIM ** -0.5
    nsel = (t0 + tq + sc - 1) // sc

    @pl.when(i == 0)
    def _():
        lane = lax.broadcasted_iota(I32, (sc, LANES), 1)
        for u in range(nsc):
            kblk = k_ref[0, u * sc:(u + 1) * sc, :]
            vblk = v_ref[0, u * sc:(u + 1) * sc, :]
            for kh in range(KV_HEADS):
                kb_ref[kh, u] = kblk[:, kh * ATT_DIM:(kh + 1) * ATT_DIM].astype(BF16)
                vt_ref[kh, u] = vblk[:, kh * ATT_DIM:(kh + 1) * ATT_DIM].astype(F32).T.astype(BF16)
            kw = kwa_ref[0, u * sc:(u + 1) * sc, :].astype(F32)
            ka = jnp.where(lane < IDX_DIM, kw, 0.0)
            kia_ref[u] = ka.astype(BF16)
            kib_ref[u] = pltpu.roll(ka, IDX_DIM, 1).astype(BF16)

    @pl.when((bi == 0) & (i == 0))
    def _():
        bias_ref[...] = jnp.zeros(bias_ref.shape, F32)

        def fill(bu, c):
            for r in range(2):
                hit = bk_ref[r] == bu
                for h in range(ATT_HEADS):
                    kh, g = divmod(h, GROUP)
                    val = (rb_ref[bu, h] - rb_ref[REL_BUCKETS - 1, h]) * log2e
                    rows = slice((2 + r) * LANES, (3 + r) * LANES)
                    cur = bias_ref[kh, rows, g * tq:(g + 1) * tq]
                    bias_ref[kh, rows, g * tq:(g + 1) * tq] = jnp.where(hit, val, cur)
            return c

        lax.fori_loop(0, REL_BUCKETS, fill, 0)

    for h in range(ATT_HEADS):
        kh, g = divmod(h, GROUP)
        qh = q_ref[0, :, h * ATT_DIM:(h + 1) * ATT_DIM].astype(F32)
        qs_ref[kh, :, g * tq:(g + 1) * tq] = qh.T.astype(BF16)

    row_s = lax.broadcasted_iota(I32, (sc, tq), 0)
    tidx = t0 + lax.broadcasted_iota(I32, (sc, tq), 1)

    @pl.when(i < nskip)
    def _():
        for u in range((nskip * tq + sc - 1) // sc):
            negm_ref[u] = jnp.where(u * sc + row_s <= tidx, 0.0, NEG_INF)

    @pl.when(i >= nskip)
    def _():
        for pp in range(IDX_HEADS // 4):
            for r in range(2):
                cols = slice((2 * pp + r) * LANES, (2 * pp + r + 1) * LANES)
                qst_ref[pp, :, r * tq:(r + 1) * tq] = qi_ref[0, :, cols].astype(F32).T.astype(BF16)
        wt = kwq_ref[0].T * idx_scale

        def head_w(h):
            return wt[IDX_DIM + h:IDX_DIM + h + 1, :]

        def score_unit(u, carry):
            ka = kia_ref[u]
            kb_ = kib_ref[u]
            acc = jnp.zeros((sc, tq), F32)
            for pp in range(IDX_HEADS // 4):
                q2 = qst_ref[pp]
                sa = jnp.maximum(_dot(ka, q2), 0.0)
                sb = jnp.maximum(_dot(kb_, q2), 0.0)
                acc = acc + sa[:, :tq] * head_w(4 * pp) + sb[:, :tq] * head_w(4 * pp + 1)
                acc = acc + sa[:, tq:] * head_w(4 * pp + 2) + sb[:, tq:] * head_w(4 * pp + 3)
            score = jnp.where(u * sc + row_s <= tidx, acc + 0.0, NEG_INF)
            bits = lax.bitcast_convert_type(score, I32)
            key = jnp.where(bits < 0, bits ^ jnp.int32(0x7FFFFFFF), bits)
            key_ref[u] = key
            hi_ref[u] = lax.shift_right_arithmetic(key, 16).astype(I16)
            lo_ref[u] = ((key & 0xFFFF) - HALF16).astype(I16)
            return carry

        lax.fori_loop(0, nsel, score_unit, 0)

        @pl.when(nsel % 2 == 1)
        def _():
            hi_ref[nsel] = jnp.full((sc, tq), -HALF16, I16)
            lo_ref[nsel] = jnp.full((sc, tq), -HALF16, I16)

        npair = (nsel + 1) // 2
        rows16 = 2 * SUBLANES
        one16, zero16 = jnp.int16(1), jnp.int16(0)

        def count16(ref, cand):
            c16 = jnp.broadcast_to(cand.astype(I16), (rows16, tq))

            def body(pi, accs):
                accs = list(accs)
                for w in range(2):
                    kk = ref[2 * pi + w]
                    for r in range(sc // rows16):
                        hit = jnp.where(kk[r * rows16:(r + 1) * rows16] >= c16, one16, zero16)
                        accs[r % len(accs)] = accs[r % len(accs)] + hit
                return tuple(accs)

            accs = lax.fori_loop(0, npair, body,
                                 tuple(jnp.zeros((rows16, tq), I16) for _ in range(4)))
            tot = (accs[0] + accs[1]) + (accs[2] + accs[3])
            return tot.astype(I32).sum(axis=0, keepdims=True)

        def search16(ref):
            def step(it, thr):
                cand = thr + lax.shift_left(jnp.int32(1), 15 - it)
                return jnp.where(count16(ref, cand) >= topk, cand, thr)
            return lax.fori_loop(0, 16, step, jnp.full((1, tq), -HALF16, I32))

        t_hi = search16(hi_ref)
        t_hi16 = jnp.broadcast_to(t_hi.astype(I16), (rows16, tq))

        def fold(pi, carry):
            for w in range(2):
                u = 2 * pi + w
                for r in range(sc // rows16):
                    rs = slice(r * rows16, (r + 1) * rows16)
                    hi = hi_ref[u, rs, :]
                    lo_ref[u, rs, :] = jnp.where(
                        hi > t_hi16, jnp.int16(HALF16 - 1),
                        jnp.where(hi == t_hi16, lo_ref[u, rs, :], jnp.int16(-HALF16)))
            return carry

        lax.fori_loop(0, npair, fold, 0)
        t_lo = search16(lo_ref)
        thr = t_hi * (2 * HALF16) + (t_lo + HALF16)

        def count(pred):
            def body(c2, acc8):
                hit = pred(key_ref[c2], c2 * sc + row_s).astype(I32)
                return acc8 + hit.reshape(sc // SUBLANES, SUBLANES, tq).sum(axis=0)
            acc8 = lax.fori_loop(0, nsel, body, jnp.zeros((SUBLANES, tq), I32))
            return acc8.sum(axis=0, keepdims=True)

        need = topk - count(lambda kk, sidx: kk > thr)
        n_tie = count(lambda kk, sidx: kk == thr)

        nbits = int(seq).bit_length()
        pos_ref[...] = jnp.full((1, tq), 2 ** nbits, I32)

        @pl.when(jnp.max(jnp.where(n_tie != need, 1, 0)) > 0)
        def _():
            def tie_step(it, pos):
                cand = pos + lax.shift_left(jnp.int32(1), nbits - 1 - it)
                cnt = count(lambda kk, sidx: (kk == thr) & (sidx < cand))
                return jnp.where(cnt <= need, cand, pos)

            pos_ref[...] = lax.fori_loop(0, nbits, tie_step, jnp.zeros((1, tq), I32))

        pos = pos_ref[...]

        def mask_unit(u, carry):
            kk = key_ref[u]
            sidx = u * sc + row_s
            sel = (kk > thr) | ((kk == thr) & (sidx < pos))
            negm_ref[u] = jnp.where(sel & (sidx <= tidx), 0.0, NEG_INF)
            return carry

        lax.fori_loop(0, nsel, mask_unit, 0)

    nfar = jnp.maximum((i - 1) // 2, 0)
    part = (sc // SUBLANES, SUBLANES, gt)

    def logits_unit(u, mparts, with_bias):
        nm = negm_ref[u]
        nm = jnp.concatenate([nm] * GROUP, axis=1)
        out = []
        for kh in range(KV_HEADS):
            s = _dot(kb_ref[kh, u], qs_ref[kh]) * att_scale + nm
            if with_bias:
                start = pl.multiple_of((2 * u - i + 3) * LANES, LANES)
                s = s + bias_ref[kh, pl.ds(start, sc), :]
            s_ref[kh, u] = s
            out.append(jnp.maximum(mparts[kh], s.reshape(part).max(axis=0)))
        return tuple(out)

    mparts = tuple(jnp.full((SUBLANES, gt), -jnp.inf, F32) for _ in range(KV_HEADS))
    mparts = lax.fori_loop(0, nfar, lambda u, c: logits_unit(u, c, False), mparts)
    mparts = lax.fori_loop(nfar, nsel, lambda u, c: logits_unit(u, c, True), mparts)
    for kh in range(KV_HEADS):
        m_ref[kh] = jnp.broadcast_to(mparts[kh].max(axis=0, keepdims=True), (SUBLANES, gt))
    acc_ref[...] = jnp.zeros(acc_ref.shape, F32)

    def probs_unit(u, lparts):
        out = []
        for kh in range(KV_HEADS):
            p = jnp.exp2(s_ref[kh, u].reshape(part) - m_ref[kh][None])
            out.append(lparts[kh] + p.sum(axis=0))
            acc_ref[kh] += _dot(vt_ref[kh, u], p.reshape(sc, gt).astype(BF16))
        return tuple(out)

    lparts = tuple(jnp.zeros((SUBLANES, gt), F32) for _ in range(KV_HEADS))
    lparts = lax.fori_loop(0, nsel, probs_unit, lparts)
    for kh in range(KV_HEADS):
        out_t = acc_ref[kh] / lparts[kh].sum(axis=0, keepdims=True)
        for g in range(GROUP):
            h = kh * GROUP + g
            o_ref[0, :, h * ATT_DIM:(h + 1) * ATT_DIM] = (
                out_t[:, g * tq:(g + 1) * tq].T.astype(o_ref.dtype))


def dsa_attention(proj, kw_f32, rel_bias, *, seq, tq=LANES):
    assert tq == LANES
    b = proj.shape[0]
    topk = min(TOPK_MAX, seq // 4)
    assert topk % tq == 0 and seq % (2 * LANES) == 0
    o_q = ATT_HEADS * ATT_DIM
    o_kv = KV_HEADS * ATT_DIM
    o_qi = IDX_HEADS * IDX_DIM
    sc = 2 * LANES
    nsc = seq // sc
    gt = GROUP * tq
    jj, ii = np.meshgrid(np.arange(tq), np.arange(tq), indexing="ij")
    buckets = jnp.asarray(np.stack([_t5_bucket_np(tq + ii - jj), _t5_bucket_np(ii - jj)]))
    once = pl.Buffered(1)
    in_specs = [
        pl.BlockSpec(memory_space=pltpu.SMEM),
        pl.BlockSpec((2, tq, tq), lambda bi, i: (0, 0, 0)),
        pl.BlockSpec((1, tq, o_q), lambda bi, i: (bi, i, 0)),
        pl.BlockSpec((1, seq, o_kv), lambda bi, i: (bi, 0, o_q // o_kv), pipeline_mode=once),
        pl.BlockSpec((1, seq, o_kv), lambda bi, i: (bi, 0, o_q // o_kv + 1), pipeline_mode=once),
        pl.BlockSpec((1, tq, o_qi), lambda bi, i: (bi, i, (o_q + 2 * o_kv) // o_qi)),
        pl.BlockSpec((1, seq, LANES), lambda bi, i: (bi, 0, (o_q + 2 * o_kv + o_qi) // LANES),
                     pipeline_mode=once),
        pl.BlockSpec((1, tq, LANES), lambda bi, i: (bi, i, 0)),
    ]
    scratch = [
        pltpu.VMEM((KV_HEADS, nsc, sc, ATT_DIM), BF16),
        pltpu.VMEM((KV_HEADS, nsc, ATT_DIM, sc), BF16),
        pltpu.VMEM((nsc, sc, LANES), BF16),
        pltpu.VMEM((nsc, sc, LANES), BF16),
        pltpu.VMEM((KV_HEADS, ATT_DIM, gt), BF16),
        pltpu.VMEM((IDX_HEADS // 4, LANES, 2 * tq), BF16),
        pltpu.VMEM((nsc, sc, tq), I32),
        pltpu.VMEM((nsc, sc, tq), I16),
        pltpu.VMEM((nsc, sc, tq), I16),
        pltpu.VMEM((nsc, sc, tq), F32),
        pltpu.VMEM((KV_HEADS, 5 * LANES, gt), F32),
        pltpu.VMEM((KV_HEADS, nsc, sc, gt), F32),
        pltpu.VMEM((KV_HEADS, SUBLANES, gt), F32),
        pltpu.VMEM((KV_HEADS, ATT_DIM, gt), F32),
        pltpu.VMEM((1, tq), I32),
    ]
    return pl.pallas_call(
        functools.partial(_dsa_kernel, seq=seq, tq=tq, topk=topk),
        grid=(b, seq // tq),
        in_specs=in_specs,
        out_specs=pl.BlockSpec((1, tq, o_q), lambda bi, i: (bi, i, 0)),
        out_shape=jax.ShapeDtypeStruct((b, seq, o_q), BF16),
        scratch_shapes=scratch,
        compiler_params=_cparams(2, 56),
        name="dsa_attention",
    )(rel_bias, buckets, proj, proj, proj, proj, proj, kw_f32)


def kernel(x, norm_mix, norm_ffn, e_w_in, e_ret_gn, e_conv_w, e_conv_b, e_gate_a_w, e_gate_a_b,
           e_gate_x_w, e_gate_x_b, e_lambda, e_w_out, o_w_in, o_w_out, rel_bias,
           ffn_w_gu, ffn_conv_w, ffn_conv_b, ffn_w_down, final_norm):
    b, s, d = x.shape
    m = b * s
    x2 = x.reshape(m, d)

    e_in = e_w_in.shape[-1]
    proj = norm_matmul(x2, norm_mix[0], e_w_in[0].astype(BF16), tm=PROJ_TM, tn=PROJ_TN)
    proj = proj.reshape(b, s, e_in)
    ret = retention(proj, e_ret_gn[0], seq=s)
    lru = rglru(proj, e_conv_w[0], e_conv_b[0], e_gate_a_w[0], e_gate_a_b[0], e_gate_x_w[0],
                e_gate_x_b[0], e_lambda[0], seq=s,
                x_off=4 * RET_HEADS, y_off=4 * RET_HEADS + LRU_BLOCKS)
    x2 = matmul_residual(x2, ret.reshape(m, RET_WIDTH), 0, lru.reshape(m, LRU_WIDTH), 0,
                         e_w_out[0].astype(BF16), tm=OUT_TM)
    x2 = conv_ffn(x2, norm_ffn[0], ffn_w_gu[0].astype(BF16), ffn_conv_w[0], ffn_conv_b[0],
                  ffn_w_down[0].astype(BF16), final_norm, seq=s, tm=FFN_TM, tf=FFN_TF,
                  final_norm=False)

    o_in = o_w_in.shape[-1]
    n_pad = -o_in % PROJ_TN
    w_in = jnp.pad(o_w_in[0].astype(BF16), ((0, 0), (0, n_pad)))
    kw_col = ATT_HEADS * ATT_DIM + 2 * KV_HEADS * ATT_DIM + IDX_HEADS * IDX_DIM
    proj, kw_f32 = norm_matmul(x2, norm_mix[1], w_in, tm=PROJ_TM, tn=PROJ_TN, out_dtype=BF16,
                               side_col=kw_col)
    attn = dsa_attention(proj.reshape(b, s, o_in + n_pad), kw_f32.reshape(b, s, LANES),
                         rel_bias, seq=s)
    attn = attn.reshape(m, ATT_HEADS * ATT_DIM)
    x2 = matmul_residual(x2, attn, 0, attn, 1, o_w_out[0].astype(BF16), tm=OUT_TM)
    x2 = conv_ffn(x2, norm_ffn[1], ffn_w_gu[1].astype(BF16), ffn_conv_w[1], ffn_conv_b[1],
                  ffn_w_down[1].astype(BF16), final_norm, seq=s, tm=FFN_TM, tf=FFN_TF,
                  final_norm=True)
    return x2.reshape(b, s, d)
```

```python
import functools
import math

import numpy as np
import jax
import jax.numpy as jnp
from jax import lax
from jax.experimental import pallas as pl
from jax.experimental.pallas import tpu as pltpu

F32 = jnp.float32
BF16 = jnp.bfloat16
I32 = jnp.int32
I16 = jnp.int16

EPS = 1e-6
NEG_INF = -1e30
ROPE_BASE = 10000.0

RET_HEADS = 8
RET_DIM = 128
RET_WIDTH = RET_HEADS * RET_DIM
RET_CHUNK = 128
LRU_BLOCKS = 8
LRU_BLOCK_DIM = 128
LRU_WIDTH = LRU_BLOCKS * LRU_BLOCK_DIM
LRU_CONV = 4
LRU_C = 8.0

ATT_HEADS = 16
ATT_DIM = 128
KV_HEADS = 4
GROUP = ATT_HEADS // KV_HEADS
IDX_HEADS = 16
IDX_DIM = 64
TOPK_MAX = 256
REL_BUCKETS = 32
REL_MAX_DIST = 128
FFN_CONV = 3

LANES = 128
SUBLANES = 8
MIB = 2 ** 20
HALF16 = 2 ** 15

PROJ_TM, PROJ_TN = 1024, 1536
CAST_BLOCK_BYTES = 6 * MIB
OUT_TM = 512
FFN_TM, FFN_TF = 1024, 512


def _cparams(n_axes, vmem_mib):
    return pltpu.CompilerParams(
        dimension_semantics=("arbitrary",) * n_axes,
        vmem_limit_bytes=int(vmem_mib * MIB))


def _dot(a, b):
    return jnp.dot(a, b, preferred_element_type=F32)


def _dot_nt(a, b):
    return lax.dot_general(a, b, (((1,), (1,)), ((), ())), preferred_element_type=F32)


def _dot_tn(a, b):
    return lax.dot_general(a, b, (((0,), (0,)), ((), ())), preferred_element_type=F32)


def _rms_rows(x, g):
    ms = jnp.mean(x * x, axis=-1, keepdims=True)
    return x * lax.rsqrt(ms + EPS) * g


def _rms_to_ref(x_ref, g_ref, h_ref, row_chunk, copy_ref=None):
    g = g_ref[...]

    def body(r, c):
        sl = pl.ds(pl.multiple_of(r * row_chunk, row_chunk), row_chunk)
        x = x_ref[sl, :]
        h_ref[sl, :] = _rms_rows(x, g).astype(h_ref.dtype)
        if copy_ref is not None:
            copy_ref[sl, :] = x
        return c

    lax.fori_loop(0, x_ref.shape[0] // row_chunk, body, 0)


def _cast_kernel(x_ref, o_ref):
    o_ref[...] = x_ref[...].astype(o_ref.dtype)


def cast_layer(w, layer):
    _, r, c = w.shape
    tr = max(CAST_BLOCK_BYTES // (c * 4) // 16 * 16, 16)
    while r % tr:
        tr -= 16
    vmem = 2 * tr * c * (4 + 2) / MIB + 4
    return pl.pallas_call(
        _cast_kernel,
        grid=(r // tr,),
        in_specs=[pl.BlockSpec((None, tr, c), lambda i: (layer, i, 0))],
        out_specs=pl.BlockSpec((tr, c), lambda i: (i, 0)),
        out_shape=jax.ShapeDtypeStruct((r, c), BF16),
        compiler_params=_cparams(1, vmem),
        name="cast_layer",
    )(w)


def _norm_matmul_kernel(x_ref, g_ref, w_ref, o_ref, *rest, row_chunk, side_col):
    h_ref = rest[-1]
    j = pl.program_id(1)

    @pl.when(j == 0)
    def _():
        _rms_to_ref(x_ref, g_ref, h_ref, row_chunk)

    res = _dot(h_ref[...], w_ref[...])
    o_ref[...] = res.astype(o_ref.dtype)
    if side_col is not None:
        side_ref = rest[0]
        tn = w_ref.shape[1]

        @pl.when(j == side_col // tn)
        def _():
            off = side_col % tn
            side_ref[...] = res[:, off:off + LANES]


def norm_matmul(x, g, w, *, tm, tn, out_dtype=F32, side_col=None, row_chunk=128):
    m, d = x.shape
    n = w.shape[1]
    out_bytes = jnp.dtype(out_dtype).itemsize
    vmem = (2 * tm * d * 4 + 2 * d * tn * 2 + 2 * tm * tn * out_bytes + tm * d * 2) / MIB + 8
    out_specs = pl.BlockSpec((tm, tn), lambda i, j: (i, j))
    out_shape = jax.ShapeDtypeStruct((m, n), out_dtype)
    if side_col is not None:
        out_specs = [out_specs, pl.BlockSpec((tm, LANES), lambda i, j: (i, 0))]
        out_shape = [out_shape, jax.ShapeDtypeStruct((m, LANES), F32)]
    return pl.pallas_call(
        functools.partial(_norm_matmul_kernel, row_chunk=row_chunk, side_col=side_col),
        grid=(m // tm, n // tn),
        in_specs=[
            pl.BlockSpec((tm, d), lambda i, j: (i, 0)),
            pl.BlockSpec((1, d), lambda i, j: (0, 0)),
            pl.BlockSpec((d, tn), lambda i, j: (0, j)),
        ],
        out_specs=out_specs,
        out_shape=out_shape,
        scratch_shapes=[pltpu.VMEM((tm, d), BF16)],
        compiler_params=_cparams(2, vmem),
        name="norm_matmul",
    )(x, g.reshape(1, d), w)


def _matmul_residual_kernel(x_ref, a1_ref, a2_ref, w1_ref, w2_ref, o_ref):
    o_ref[...] = x_ref[...] + _dot(a1_ref[...], w1_ref[...]) + _dot(a2_ref[...], w2_ref[...])


def matmul_residual(x, a1, c1, a2, c2, w, *, tm):
    m, n = x.shape
    k1 = w.shape[0] // 2
    vmem = (4 * tm * n * 4 + 4 * tm * k1 * 2 + 4 * k1 * n * 2) / MIB + 8
    return pl.pallas_call(
        _matmul_residual_kernel,
        grid=(m // tm,),
        in_specs=[
            pl.BlockSpec((tm, n), lambda i: (i, 0)),
            pl.BlockSpec((tm, k1), lambda i: (i, c1)),
            pl.BlockSpec((tm, k1), lambda i: (i, c2)),
            pl.BlockSpec((k1, n), lambda i: (0, 0)),
            pl.BlockSpec((k1, n), lambda i: (1, 0)),
        ],
        out_specs=pl.BlockSpec((tm, n), lambda i: (i, 0)),
        out_shape=jax.ShapeDtypeStruct((m, n), F32),
        compiler_params=_cparams(1, vmem),
        name="matmul_residual",
    )(x, a1, a2, w, w)


def _ffn_kernel(x_ref, g_ref, wg_ref, wu_ref, cw_ref, cb_ref, wd_ref, gf_ref, o_ref,
                h_ref, carry_ref, *, tm, seq, row_chunk, final_norm):
    i = pl.program_id(0)
    j = pl.program_id(1)
    nj = pl.num_programs(1)

    @pl.when(j == 0)
    def _():
        _rms_to_ref(x_ref, g_ref, h_ref, row_chunk, copy_ref=o_ref)

    h = h_ref[...]
    g = _dot(h, wg_ref[...])
    u = _dot(h, wu_ref[...])

    seq_start = (i * tm) % seq == 0
    prev = jnp.where(seq_start, 0.0, carry_ref[j])
    carry_ref[j] = g[tm - SUBLANES:, :]
    row = lax.broadcasted_iota(I32, (SUBLANES, g.shape[1]), 0)
    g1 = pltpu.roll(g, 1, 0)
    g2 = pltpu.roll(g, 2, 0)
    g1_top = jnp.where(row == 0, prev[7:8, :], g1[:SUBLANES])
    g2_top = jnp.where(row == 0, prev[6:7, :], jnp.where(row == 1, prev[7:8, :], g2[:SUBLANES]))
    g1 = jnp.concatenate([g1_top, g1[SUBLANES:]], axis=0)
    g2 = jnp.concatenate([g2_top, g2[SUBLANES:]], axis=0)
    cw = cw_ref[...]
    gc = cb_ref[...] + cw[0:1] * g2 + cw[1:2] * g1 + cw[2:3] * g
    act = (gc * jax.nn.sigmoid(gc) * u).astype(BF16)
    o_ref[...] += _dot(act, wd_ref[...])

    if final_norm:
        @pl.when(j == nj - 1)
        def _():
            _rms_to_ref(o_ref, gf_ref, o_ref, row_chunk)


def conv_ffn(x, g, w_gu, conv_w, conv_b, w_down, g_final, *, seq, tm, tf,
             final_norm, row_chunk=128):
    m, d = x.shape
    f = w_down.shape[0]
    nf = f // tf
    vmem = (4 * tm * d * 4 + tm * d * 2 + 4 * d * tf * 2 + 2 * tf * d * 2) / MIB + 8
    return pl.pallas_call(
        functools.partial(_ffn_kernel, tm=tm, seq=seq, row_chunk=row_chunk,
                          final_norm=final_norm),
        grid=(m // tm, nf),
        in_specs=[
            pl.BlockSpec((tm, d), lambda i, j: (i, 0)),
            pl.BlockSpec((1, d), lambda i, j: (0, 0)),
            pl.BlockSpec((d, tf), lambda i, j: (0, j)),
            pl.BlockSpec((d, tf), lambda i, j: (0, j + nf)),
            pl.BlockSpec((FFN_CONV, tf), lambda i, j: (0, j)),
            pl.BlockSpec((1, tf), lambda i, j: (0, j)),
            pl.BlockSpec((tf, d), lambda i, j: (j, 0)),
            pl.BlockSpec((1, d), lambda i, j: (0, 0)),
        ],
        out_specs=pl.BlockSpec((tm, d), lambda i, j: (i, 0)),
        out_shape=jax.ShapeDtypeStruct((m, d), F32),
        scratch_shapes=[pltpu.VMEM((tm, d), BF16),
                        pltpu.VMEM((nf, SUBLANES, tf), F32)],
        compiler_params=_cparams(2, vmem),
        name="conv_ffn",
    )(x, g.reshape(1, d), w_gu, w_gu, conv_w, conv_b.reshape(1, f), w_down,
      g_final.reshape(1, d))


def _retention_kernel(q_ref, k_ref, v_ref, gate_ref, cos_ref, sin_ref, dec_ref, qd_ref,
                      kd_ref, cd_ref, gn_ref, o_ref, state_ref, *, seq, hp):
    c = RET_CHUNK
    cpi = 4
    scale = RET_DIM ** -0.5
    state_ref[...] = jnp.zeros(state_ref.shape, F32)

    def chunk_group(gi, carry):
        sls = [pl.ds(pl.multiple_of((gi * cpi + cc) * c, c), c) for cc in range(cpi)]
        cos = [cos_ref[sl, :] for sl in sls]
        sin = [sin_ref[sl, :] for sl in sls]
        for hh in range(hp):
            cols = slice(hh * RET_DIM, (hh + 1) * RET_DIM)
            state = state_ref[hh]
            for cc, sl in enumerate(sls):
                q = q_ref[0, sl, cols]
                k = k_ref[0, sl, cols]
                v = v_ref[0, sl, cols].astype(BF16)
                qr = q * cos[cc] + pltpu.roll(q, RET_DIM // 2, 1) * sin[cc]
                kr = (k * cos[cc] + pltpu.roll(k, RET_DIM // 2, 1) * sin[cc]) * scale
                scores = _dot_nt(qr.astype(BF16), kr.astype(BF16)) * dec_ref[hh]
                y = _dot(scores.astype(BF16), v)
                y = y + _dot((qr * qd_ref[hh]).astype(BF16), state.astype(BF16))
                kv = _dot_tn((kr * kd_ref[hh]).astype(BF16), v)
                state = cd_ref[hh][0:1, :] * state + kv
                y = y * lax.rsqrt(jnp.mean(y * y, axis=-1, keepdims=True) + EPS)
                y = y * gn_ref[:, cols]
                gate = gate_ref[0, sl, cols]
                o_ref[0, sl, cols] = (y * (gate * jax.nn.sigmoid(gate))).astype(o_ref.dtype)
            state_ref[hh] = state
        return carry

    lax.fori_loop(0, seq // (c * cpi), chunk_group, 0)


def retention(proj, gn, *, seq, hp=4):
    b = proj.shape[0]
    h_ = RET_HEADS
    c = RET_CHUNK
    half = RET_DIM // 2
    freqs = ROPE_BASE ** (-jnp.arange(half, dtype=F32) / half)
    ang = jnp.arange(seq, dtype=F32)[:, None] * freqs[None, :]
    cos2 = jnp.concatenate([jnp.cos(ang), jnp.cos(ang)], axis=-1)
    sin2 = jnp.concatenate([-jnp.sin(ang), jnp.sin(ang)], axis=-1)
    log_g = jnp.log1p(-jnp.exp2(-5.0 - jnp.arange(h_, dtype=F32)))
    pos = jnp.arange(c, dtype=F32)
    diff = pos[:, None] - pos[None, :]
    inner = jnp.where(diff[None] >= 0,
                      jnp.exp(jnp.maximum(diff, 0.0)[None] * log_g[:, None, None]), 0.0)
    q_decay = jnp.exp((pos[None, :] + 1.0) * log_g[:, None])
    k_decay = jnp.exp((c - 1.0 - pos[None, :]) * log_g[:, None])
    chunk_decay = jnp.exp(c * log_g)
    qd = jnp.broadcast_to(q_decay[:, :, None], (h_, c, LANES))
    kd = jnp.broadcast_to(k_decay[:, :, None], (h_, c, LANES))
    cd = jnp.broadcast_to(chunk_decay[:, None, None], (h_, SUBLANES, LANES))

    ng = h_ // hp
    wide = hp * RET_DIM
    head_spec = lambda sec: pl.BlockSpec((1, seq, wide), lambda bi, hi: (bi, 0, sec * ng + hi))
    const_spec = pl.BlockSpec((seq, LANES), lambda bi, hi: (0, 0))
    per_head = lambda r: pl.BlockSpec((hp, r, LANES), lambda bi, hi: (hi, 0, 0))
    vmem = (2 * 4 * seq * wide * 4 + 2 * seq * wide * 2 + 4 * seq * LANES * 4) / MIB + 6
    return pl.pallas_call(
        functools.partial(_retention_kernel, seq=seq, hp=hp),
        grid=(b, ng),
        in_specs=[head_spec(0), head_spec(1), head_spec(2), head_spec(3),
                  const_spec, const_spec, per_head(c), per_head(c), per_head(c),
                  per_head(SUBLANES),
                  pl.BlockSpec((1, wide), lambda bi, hi: (0, hi))],
        out_specs=pl.BlockSpec((1, seq, wide), lambda bi, hi: (bi, 0, hi)),
        out_shape=jax.ShapeDtypeStruct((b, seq, RET_WIDTH), BF16),
        scratch_shapes=[pltpu.VMEM((hp, RET_DIM, RET_DIM), F32)],
        compiler_params=_cparams(2, vmem),
        name="retention",
    )(proj, proj, proj, proj, cos2, sin2, inner, qd, kd, cd, gn.reshape(1, RET_WIDTH))


def _shift_rows(v, d, row8):
    r = pltpu.roll(v, d, 0)
    top = jnp.where(row8 >= d, r[:SUBLANES], 0.0)
    return jnp.concatenate([top, r[SUBLANES:]], axis=0)


def _rglru_kernel(x_ref, y_ref, cw_ref, cb_ref, wa_ref, wx_ref, ba_ref, bx_ref, lam_ref,
                  o_ref, ga_ref, gb_ref, carry_ref, *, seq):
    x = x_ref[0]
    row = lax.broadcasted_iota(I32, x.shape, 0)
    cw = cw_ref[...]
    row8 = row[:SUBLANES]
    xc = cb_ref[...] + cw[0:1] * _shift_rows(x, 3, row8)
    xc = xc + cw[1:2] * _shift_rows(x, 2, row8)
    xc = xc + cw[2:3] * _shift_rows(x, 1, row8)
    xc = xc + cw[3:4] * x
    xb = xc.astype(BF16)
    r = jax.nn.sigmoid(_dot(xb, wa_ref[0].astype(BF16)) + ba_ref[...])
    ig = jax.nn.sigmoid(_dot(xb, wx_ref[0].astype(BF16)) + bx_ref[...])
    z = -lam_ref[...]
    softplus = jnp.maximum(z, 0.0) + jnp.log1p(jnp.exp(-jnp.abs(z)))
    log_a = -LRU_C * r * softplus
    a = jnp.exp(log_a)
    v = jnp.maximum(1.0 - a * a, 0.0)
    mult = jnp.where(v > 0.0, v * lax.rsqrt(v), 0.0)
    bb = mult * (ig * xc)

    def doubling(a, bb, idx, limit):
        d = 1
        while d < limit:
            ok = idx >= d
            a_sh = jnp.where(ok, pltpu.roll(a, d, 0), 1.0)
            b_sh = jnp.where(ok, pltpu.roll(bb, d, 0), 0.0)
            bb = bb + a * b_sh
            a = a * a_sh
            d *= 2
        return a, bb

    a, bb = doubling(a, bb, row & (SUBLANES - 1), SUBLANES)
    ga_ref[...] = a
    gb_ref[...] = bb
    ng = seq // SUBLANES
    last = pl.ds(SUBLANES - 1, ng, stride=SUBLANES)
    grow = lax.broadcasted_iota(I32, (ng, x.shape[1]), 0)
    _, h_end = doubling(ga_ref[last, :], gb_ref[last, :], grow, ng)
    carry_ref[...] = jnp.where(grow >= 1, pltpu.roll(h_end, 1, 0), 0.0)

    yv = y_ref[0]
    gelu = 0.5 * yv * (1.0 + jnp.tanh(math.sqrt(2.0 / math.pi) * (yv + 0.044715 * (yv * yv * yv))))
    h = jnp.concatenate(
        [bb[g * SUBLANES:(g + 1) * SUBLANES] + a[g * SUBLANES:(g + 1) * SUBLANES] * carry_ref[g:g + 1, :]
         for g in range(ng)], axis=0)
    o_ref[0] = (h * gelu).astype(o_ref.dtype)


def rglru(proj, conv_w, conv_b, wa, ba, wx, bx, lam, *, seq, x_off, y_off):
    b = proj.shape[0]
    nb = LRU_BLOCKS
    w = LRU_WIDTH
    blk = lambda off: pl.BlockSpec((1, seq, LANES), lambda bi, ni: (bi, 0, off + ni))
    vec = pl.BlockSpec((1, LANES), lambda bi, ni: (0, ni))
    mat = pl.BlockSpec((1, LRU_BLOCK_DIM, LRU_BLOCK_DIM), lambda bi, ni: (ni, 0, 0))
    return pl.pallas_call(
        functools.partial(_rglru_kernel, seq=seq),
        grid=(b, nb),
        in_specs=[blk(x_off), blk(y_off),
                  pl.BlockSpec((LRU_CONV, LANES), lambda bi, ni: (0, ni)),
                  vec, mat, mat, vec, vec, vec],
        out_specs=pl.BlockSpec((1, seq, LANES), lambda bi, ni: (bi, 0, ni)),
        out_shape=jax.ShapeDtypeStruct((b, seq, w), BF16),
        scratch_shapes=[pltpu.VMEM((seq, LANES), F32), pltpu.VMEM((seq, LANES), F32),
                        pltpu.VMEM((seq // SUBLANES, LANES), F32)],
        compiler_params=_cparams(2, 48),
        name="rglru",
    )(proj, proj, conv_w, conv_b.reshape(1, w), wa, wx, ba.reshape(1, w), bx.reshape(1, w),
      lam.reshape(1, w))


def _t5_bucket_np(rel):
    n = np.maximum(rel, 0)
    max_exact = REL_BUCKETS // 2
    nf = np.maximum(n, max_exact).astype(np.float64)
    large = max_exact + (np.log(nf / max_exact) / math.log(REL_MAX_DIST / max_exact)
                         * (REL_BUCKETS - max_exact)).astype(np.int32)
    large = np.minimum(large, REL_BUCKETS - 1)
    return np.where(n < max_exact, n, large).astype(np.int32)


def _dsa_kernel(rb_ref, bk_ref, q_ref, k_ref, v_ref, qi_ref, kwa_ref, kwq_ref, o_ref,
                kb_ref, vt_ref, kia_ref, kib_ref, qs_ref, qst_ref, key_ref, hi_ref, lo_ref,
                negm_ref, bias_ref, s_ref, m_ref, acc_ref, pos_ref, *, seq, tq, topk):
    bi = pl.program_id(0)
    i = pl.program_id(1)
    t0 = i * tq
    sc = 2 * LANES
    nsc = seq // sc
    gt = GROUP * tq
    nskip = topk // tq
    log2e = math.log2(math.e)
    att_scale = ATT_DIM ** -0.5 * log2e
    idx_scale = IDX_HEADS ** -0.5 * IDX_DIM ** -0.5
    nsel = (t0 + tq + sc - 1) // sc

    @pl.when(i == 0)
    def _():
        lane = lax.broadcasted_iota(I32, (sc, LANES), 1)
        for u in range(nsc):
            kblk = k_ref[0, u * sc:(u + 1) * sc, :]
            vblk = v_ref[0, u * sc:(u + 1) * sc, :]
            for kh in range(KV_HEADS):
                kb_ref[kh, u] = kblk[:, kh * ATT_DIM:(kh + 1) * ATT_DIM].astype(BF16)
                vt_ref[kh, u] = vblk[:, kh * ATT_DIM:(kh + 1) * ATT_DIM].astype(F32).T.astype(BF16)
            kw = kwa_ref[0, u * sc:(u + 1) * sc, :].astype(F32)
            ka = jnp.where(lane < IDX_DIM, kw, 0.0)
            kia_ref[u] = ka.astype(BF16)
            kib_ref[u] = pltpu.roll(ka, IDX_DIM, 1).astype(BF16)

    @pl.when((bi == 0) & (i == 0))
    def _():
        bias_ref[...] = jnp.zeros(bias_ref.shape, F32)

        def fill(bu, c):
            for r in range(2):
                hit = bk_ref[r] == bu
                for h in range(ATT_HEADS):
                    kh, g = divmod(h, GROUP)
                    val = (rb_ref[bu, h] - rb_ref[REL_BUCKETS - 1, h]) * log2e
                    rows = slice((2 + r) * LANES, (3 + r) * LANES)
                    cur = bias_ref[kh, rows, g * tq:(g + 1) * tq]
                    bias_ref[kh, rows, g * tq:(g + 1) * tq] = jnp.where(hit, val, cur)
            return c

        lax.fori_loop(0, REL_BUCKETS, fill, 0)

    for h in range(ATT_HEADS):
        kh, g = divmod(h, GROUP)
        qh = q_ref[0, :, h * ATT_DIM:(h + 1) * ATT_DIM].astype(F32)
        qs_ref[kh, :, g * tq:(g + 1) * tq] = qh.T.astype(BF16)

    row_s = lax.broadcasted_iota(I32, (sc, tq), 0)
    tidx = t0 + lax.broadcasted_iota(I32, (sc, tq), 1)

    @pl.when(i < nskip)
    def _():
        for u in range((nskip * tq + sc - 1) // sc):
            negm_ref[u] = jnp.where(u * sc + row_s <= tidx, 0.0, NEG_INF)

    @pl.when(i >= nskip)
    def _():
        for pp in range(IDX_HEADS // 4):
            for r in range(2):
                cols = slice((2 * pp + r) * LANES, (2 * pp + r + 1) * LANES)
                qst_ref[pp, :, r * tq:(r + 1) * tq] = qi_ref[0, :, cols].astype(F32).T.astype(BF16)
        wt = kwq_ref[0].T * idx_scale

        def head_w(h):
            return wt[IDX_DIM + h:IDX_DIM + h + 1, :]

        def score_unit(u, carry):
            ka = kia_ref[u]
            kb_ = kib_ref[u]
            acc = jnp.zeros((sc, tq), F32)
            for pp in range(IDX_HEADS // 4):
                q2 = qst_ref[pp]
                sa = jnp.maximum(_dot(ka, q2), 0.0)
                sb = jnp.maximum(_dot(kb_, q2), 0.0)
                acc = acc + sa[:, :tq] * head_w(4 * pp) + sb[:, :tq] * head_w(4 * pp + 1)
                acc = acc + sa[:, tq:] * head_w(4 * pp + 2) + sb[:, tq:] * head_w(4 * pp + 3)
            score = jnp.where(u * sc + row_s <= tidx, acc + 0.0, NEG_INF)
            bits = lax.bitcast_convert_type(score, I32)
            key = jnp.where(bits < 0, bits ^ jnp.int32(0x7FFFFFFF), bits)
            key_ref[u] = key
            hi_ref[u] = lax.shift_right_arithmetic(key, 16).astype(I16)
            lo_ref[u] = ((key & 0xFFFF) - HALF16).astype(I16)
            return carry

        lax.fori_loop(0, nsel, score_unit, 0)

        @pl.when(nsel % 2 == 1)
        def _():
            hi_ref[nsel] = jnp.full((sc, tq), -HALF16, I16)
            lo_ref[nsel] = jnp.full((sc, tq), -HALF16, I16)

        npair = (nsel + 1) // 2
        rows16 = 2 * SUBLANES
        one16, zero16 = jnp.int16(1), jnp.int16(0)

        def count16(ref, cand):
            c16 = jnp.broadcast_to(cand.astype(I16), (rows16, tq))

            def body(pi, accs):
                accs = list(accs)
                for w in range(2):
                    kk = ref[2 * pi + w]
                    for r in range(sc // rows16):
                        hit = jnp.where(kk[r * rows16:(r + 1) * rows16] >= c16, one16, zero16)
                        accs[r % len(accs)] = accs[r % len(accs)] + hit
                return tuple(accs)

            accs = lax.fori_loop(0, npair, body,
                                 tuple(jnp.zeros((rows16, tq), I16) for _ in range(4)))
            tot = (accs[0] + accs[1]) + (accs[2] + accs[3])
            return tot.astype(I32).sum(axis=0, keepdims=True)

        def search16(ref):
            def step(it, thr):
                cand = thr + lax.shift_left(jnp.int32(1), 15 - it)
                return jnp.where(count16(ref, cand) >= topk, cand, thr)
            return lax.fori_loop(0, 16, step, jnp.full((1, tq), -HALF16, I32))

        t_hi = search16(hi_ref)
        t_hi16 = jnp.broadcast_to(t_hi.astype(I16), (rows16, tq))

        def fold(pi, carry):
            for w in range(2):
                u = 2 * pi + w
                for r in range(sc // rows16):
                    rs = slice(r * rows16, (r + 1) * rows16)
                    hi = hi_ref[u, rs, :]
                    lo_ref[u, rs, :] = jnp.where(
                        hi > t_hi16, jnp.int16(HALF16 - 1),
                        jnp.where(hi == t_hi16, lo_ref[u, rs, :], jnp.int16(-HALF16)))
            return carry

        lax.fori_loop(0, npair, fold, 0)
        t_lo = search16(lo_ref)
        thr = t_hi * (2 * HALF16) + (t_lo + HALF16)

        def count(pred):
            def body(c2, acc8):
                hit = pred(key_ref[c2], c2 * sc + row_s).astype(I32)
                return acc8 + hit.reshape(sc // SUBLANES, SUBLANES, tq).sum(axis=0)
            acc8 = lax.fori_loop(0, nsel, body, jnp.zeros((SUBLANES, tq), I32))
            return acc8.sum(axis=0, keepdims=True)

        need = topk - count(lambda kk, sidx: kk > thr)
        n_tie = count(lambda kk, sidx: kk == thr)

        nbits = int(seq).bit_length()
        pos_ref[...] = jnp.full((1, tq), 2 ** nbits, I32)

        @pl.when(jnp.max(jnp.where(n_tie != need, 1, 0)) > 0)
        def _():
            def tie_step(it, pos):
                cand = pos + lax.shift_left(jnp.int32(1), nbits - 1 - it)
                cnt = count(lambda kk, sidx: (kk == thr) & (sidx < cand))
                return jnp.where(cnt <= need, cand, pos)

            pos_ref[...] = lax.fori_loop(0, nbits, tie_step, jnp.zeros((1, tq), I32))

        pos = pos_ref[...]

        def mask_unit(u, carry):
            kk = key_ref[u]
            sidx = u * sc + row_s
            sel = (kk > thr) | ((kk == thr) & (sidx < pos))
            negm_ref[u] = jnp.where(sel & (sidx <= tidx), 0.0, NEG_INF)
            return carry

        lax.fori_loop(0, nsel, mask_unit, 0)

    nfar = jnp.maximum((i - 1) // 2, 0)
    part = (sc // SUBLANES, SUBLANES, gt)

    def logits_unit(u, mparts, with_bias):
        nm = negm_ref[u]
        nm = jnp.concatenate([nm] * GROUP, axis=1)
        out = []
        for kh in range(KV_HEADS):
            s = _dot(kb_ref[kh, u], qs_ref[kh]) * att_scale + nm
            if with_bias:
                start = pl.multiple_of((2 * u - i + 3) * LANES, LANES)
                s = s + bias_ref[kh, pl.ds(start, sc), :]
            s_ref[kh, u] = s
            out.append(jnp.maximum(mparts[kh], s.reshape(part).max(axis=0)))
        return tuple(out)

    mparts = tuple(jnp.full((SUBLANES, gt), -jnp.inf, F32) for _ in range(KV_HEADS))
    mparts = lax.fori_loop(0, nfar, lambda u, c: logits_unit(u, c, False), mparts)
    mparts = lax.fori_loop(nfar, nsel, lambda u, c: logits_unit(u, c, True), mparts)
    for kh in range(KV_HEADS):
        m_ref[kh] = jnp.broadcast_to(mparts[kh].max(axis=0, keepdims=True), (SUBLANES, gt))
    acc_ref[...] = jnp.zeros(acc_ref.shape, F32)

    def probs_unit(u, lparts):
        out = []
        for kh in range(KV_HEADS):
            p = jnp.exp2(s_ref[kh, u].reshape(part) - m_ref[kh][None])
            out.append(lparts[kh] + p.sum(axis=0))
            acc_ref[kh] += _dot(vt_ref[kh, u], p.reshape(sc, gt).astype(BF16))
        return tuple(out)

    lparts = tuple(jnp.zeros((SUBLANES, gt), F32) for _ in range(KV_HEADS))
    lparts = lax.fori_loop(0, nsel, probs_unit, lparts)
    for kh in range(KV_HEADS):
        out_t = acc_ref[kh] / lparts[kh].sum(axis=0, keepdims=True)
        for g in range(GROUP):
            h = kh * GROUP + g
            o_ref[0, :, h * ATT_DIM:(h + 1) * ATT_DIM] = (
                out_t[:, g * tq:(g + 1) * tq].T.astype(o_ref.dtype))


def dsa_attention(proj, kw_f32, rel_bias, *, seq, tq=LANES):
    assert tq == LANES
    b = proj.shape[0]
    topk = min(TOPK_MAX, seq // 4)
    assert topk % tq == 0 and seq % (2 * LANES) == 0
    o_q = ATT_HEADS * ATT_DIM
    o_kv = KV_HEADS * ATT_DIM
    o_qi = IDX_HEADS * IDX_DIM
    sc = 2 * LANES
    nsc = seq // sc
    gt = GROUP * tq
    jj, ii = np.meshgrid(np.arange(tq), np.arange(tq), indexing="ij")
    buckets = jnp.asarray(np.stack([_t5_bucket_np(tq + ii - jj), _t5_bucket_np(ii - jj)]))
    once = pl.Buffered(1)
    in_specs = [
        pl.BlockSpec(memory_space=pltpu.SMEM),
        pl.BlockSpec((2, tq, tq), lambda bi, i: (0, 0, 0)),
        pl.BlockSpec((1, tq, o_q), lambda bi, i: (bi, i, 0)),
        pl.BlockSpec((1, seq, o_kv), lambda bi, i: (bi, 0, o_q // o_kv), pipeline_mode=once),
        pl.BlockSpec((1, seq, o_kv), lambda bi, i: (bi, 0, o_q // o_kv + 1), pipeline_mode=once),
        pl.BlockSpec((1, tq, o_qi), lambda bi, i: (bi, i, (o_q + 2 * o_kv) // o_qi)),
        pl.BlockSpec((1, seq, LANES), lambda bi, i: (bi, 0, (o_q + 2 * o_kv + o_qi) // LANES),
                     pipeline_mode=once),
        pl.BlockSpec((1, tq, LANES), lambda bi, i: (bi, i, 0)),
    ]
    scratch = [
        pltpu.VMEM((KV_HEADS, nsc, sc, ATT_DIM), BF16),
        pltpu.VMEM((KV_HEADS, nsc, ATT_DIM, sc), BF16),
        pltpu.VMEM((nsc, sc, LANES), BF16),
        pltpu.VMEM((nsc, sc, LANES), BF16),
        pltpu.VMEM((KV_HEADS, ATT_DIM, gt), BF16),
        pltpu.VMEM((IDX_HEADS // 4, LANES, 2 * tq), BF16),
        pltpu.VMEM((nsc, sc, tq), I32),
        pltpu.VMEM((nsc, sc, tq), I16),
        pltpu.VMEM((nsc, sc, tq), I16),
        pltpu.VMEM((nsc, sc, tq), F32),
        pltpu.VMEM((KV_HEADS, 5 * LANES, gt), F32),
        pltpu.VMEM((KV_HEADS, nsc, sc, gt), F32),
        pltpu.VMEM((KV_HEADS, SUBLANES, gt), F32),
        pltpu.VMEM((KV_HEADS, ATT_DIM, gt), F32),
        pltpu.VMEM((1, tq), I32),
    ]
    return pl.pallas_call(
        functools.partial(_dsa_kernel, seq=seq, tq=tq, topk=topk),
        grid=(b, seq // tq),
        in_specs=in_specs,
        out_specs=pl.BlockSpec((1, tq, o_q), lambda bi, i: (bi, i, 0)),
        out_shape=jax.ShapeDtypeStruct((b, seq, o_q), BF16),
        scratch_shapes=scratch,
        compiler_params=_cparams(2, 56),
        name="dsa_attention",
    )(rel_bias, buckets, proj, proj, proj, proj, proj, kw_f32)


def kernel(x, norm_mix, norm_ffn, e_w_in, e_ret_gn, e_conv_w, e_conv_b, e_gate_a_w, e_gate_a_b,
           e_gate_x_w, e_gate_x_b, e_lambda, e_w_out, o_w_in, o_w_out, rel_bias,
           ffn_w_gu, ffn_conv_w, ffn_conv_b, ffn_w_down, final_norm):
    b, s, d = x.shape
    m = b * s
    x2 = x.reshape(m, d)

    e_in = e_w_in.shape[-1]
    proj = norm_matmul(x2, norm_mix[0], cast_layer(e_w_in, 0), tm=PROJ_TM, tn=PROJ_TN)
    proj = proj.reshape(b, s, e_in)
    ret = retention(proj, e_ret_gn[0], seq=s)
    lru = rglru(proj, e_conv_w[0], e_conv_b[0], e_gate_a_w[0], e_gate_a_b[0], e_gate_x_w[0],
                e_gate_x_b[0], e_lambda[0], seq=s,
                x_off=4 * RET_HEADS, y_off=4 * RET_HEADS + LRU_BLOCKS)
    x2 = matmul_residual(x2, ret.reshape(m, RET_WIDTH), 0, lru.reshape(m, LRU_WIDTH), 0,
                         cast_layer(e_w_out, 0), tm=OUT_TM)
    x2 = conv_ffn(x2, norm_ffn[0], cast_layer(ffn_w_gu, 0), ffn_conv_w[0], ffn_conv_b[0],
                  cast_layer(ffn_w_down, 0), final_norm, seq=s, tm=FFN_TM, tf=FFN_TF,
                  final_norm=False)

    o_in = o_w_in.shape[-1]
    n_pad = -o_in % PROJ_TN
    w_in = jnp.pad(o_w_in[0].astype(BF16), ((0, 0), (0, n_pad)))
    kw_col = ATT_HEADS * ATT_DIM + 2 * KV_HEADS * ATT_DIM + IDX_HEADS * IDX_DIM
    proj, kw_f32 = norm_matmul(x2, norm_mix[1], w_in, tm=PROJ_TM, tn=PROJ_TN, out_dtype=BF16,
                               side_col=kw_col)
    attn = dsa_attention(proj.reshape(b, s, o_in + n_pad), kw_f32.reshape(b, s, LANES),
                         rel_bias, seq=s)
    attn = attn.reshape(m, ATT_HEADS * ATT_DIM)
    x2 = matmul_residual(x2, attn, 0, attn, 1, cast_layer(o_w_out, 0), tm=OUT_TM)
    x2 = conv_ffn(x2, norm_ffn[1], cast_layer(ffn_w_gu, 1), ffn_conv_w[1], ffn_conv_b[1],
                  cast_layer(ffn_w_down, 1), final_norm, seq=s, tm=FFN_TM, tf=FFN_TF,
                  final_norm=True)
    return x2.reshape(b, s, d)
```

```python
import functools
import math

import numpy as np
import jax
import jax.numpy as jnp
from jax import lax
from jax.experimental import pallas as pl
from jax.experimental.pallas import tpu as pltpu

F32 = jnp.float32
BF16 = jnp.bfloat16
I32 = jnp.int32
I16 = jnp.int16

EPS = 1e-6
NEG_INF = -1e30
ROPE_BASE = 10000.0

RET_HEADS = 8
RET_DIM = 128
RET_WIDTH = RET_HEADS * RET_DIM
RET_CHUNK = 128
LRU_BLOCKS = 8
LRU_BLOCK_DIM = 128
LRU_WIDTH = LRU_BLOCKS * LRU_BLOCK_DIM
LRU_CONV = 4
LRU_C = 8.0

ATT_HEADS = 16
ATT_DIM = 128
KV_HEADS = 4
GROUP = ATT_HEADS // KV_HEADS
IDX_HEADS = 16
IDX_DIM = 64
TOPK_MAX = 256
REL_BUCKETS = 32
REL_MAX_DIST = 128
FFN_CONV = 3

LANES = 128
SUBLANES = 8
MIB = 2 ** 20
HALF16 = 2 ** 15

PROJ_TM, PROJ_TN = 1024, 1536
CAST_BLOCK_BYTES = 6 * MIB
OUT_TM = 512
FFN_TM, FFN_TF = 1024, 512


def _cparams(n_axes, vmem_mib):
    return pltpu.CompilerParams(
        dimension_semantics=("arbitrary",) * n_axes,
        vmem_limit_bytes=int(vmem_mib * MIB))


def _dot(a, b):
    return jnp.dot(a, b, preferred_element_type=F32)


def _dot_nt(a, b):
    return lax.dot_general(a, b, (((1,), (1,)), ((), ())), preferred_element_type=F32)


def _dot_tn(a, b):
    return lax.dot_general(a, b, (((0,), (0,)), ((), ())), preferred_element_type=F32)


def _rms_rows(x, g):
    ms = jnp.mean(x * x, axis=-1, keepdims=True)
    return x * lax.rsqrt(ms + EPS) * g


def _rms_to_ref(x_ref, g_ref, h_ref, row_chunk, copy_ref=None):
    g = g_ref[...]

    def body(r, c):
        sl = pl.ds(pl.multiple_of(r * row_chunk, row_chunk), row_chunk)
        x = x_ref[sl, :]
        h_ref[sl, :] = _rms_rows(x, g).astype(h_ref.dtype)
        if copy_ref is not None:
            copy_ref[sl, :] = x
        return c

    lax.fori_loop(0, x_ref.shape[0] // row_chunk, body, 0)


def _cast_kernel(x_ref, o_ref):
    o_ref[...] = x_ref[...].astype(o_ref.dtype)


def cast_layer(w, layer):
    _, r, c = w.shape
    tr = max(CAST_BLOCK_BYTES // (c * 4) // 16 * 16, 16)
    while r % tr:
        tr -= 16
    vmem = 2 * tr * c * (4 + 2) / MIB + 4
    return pl.pallas_call(
        _cast_kernel,
        grid=(r // tr,),
        in_specs=[pl.BlockSpec((None, tr, c), lambda i: (layer, i, 0))],
        out_specs=pl.BlockSpec((tr, c), lambda i: (i, 0)),
        out_shape=jax.ShapeDtypeStruct((r, c), BF16),
        compiler_params=_cparams(1, vmem),
        name="cast_layer",
    )(w)


def _norm_matmul_kernel(x_ref, g_ref, w_ref, o_ref, *rest, row_chunk, side_col):
    h_ref = rest[-1]
    j = pl.program_id(1)

    @pl.when(j == 0)
    def _():
        _rms_to_ref(x_ref, g_ref, h_ref, row_chunk)

    res = _dot(h_ref[...], w_ref[...])
    o_ref[...] = res.astype(o_ref.dtype)
    if side_col is not None:
        side_ref = rest[0]
        tn = w_ref.shape[1]

        @pl.when(j == side_col // tn)
        def _():
            off = side_col % tn
            side_ref[...] = res[:, off:off + LANES]


def norm_matmul(x, g, w, *, tm, tn, out_dtype=F32, side_col=None, row_chunk=128):
    m, d = x.shape
    n = w.shape[1]
    out_bytes = jnp.dtype(out_dtype).itemsize
    vmem = (2 * tm * d * 4 + 2 * d * tn * 2 + 2 * tm * tn * out_bytes + tm * d * 2) / MIB + 8
    out_specs = pl.BlockSpec((tm, tn), lambda i, j: (i, j))
    out_shape = jax.ShapeDtypeStruct((m, n), out_dtype)
    if side_col is not None:
        out_specs = [out_specs, pl.BlockSpec((tm, LANES), lambda i, j: (i, 0))]
        out_shape = [out_shape, jax.ShapeDtypeStruct((m, LANES), F32)]
    return pl.pallas_call(
        functools.partial(_norm_matmul_kernel, row_chunk=row_chunk, side_col=side_col),
        grid=(m // tm, n // tn),
        in_specs=[
            pl.BlockSpec((tm, d), lambda i, j: (i, 0)),
            pl.BlockSpec((1, d), lambda i, j: (0, 0)),
            pl.BlockSpec((d, tn), lambda i, j: (0, j)),
        ],
        out_specs=out_specs,
        out_shape=out_shape,
        scratch_shapes=[pltpu.VMEM((tm, d), BF16)],
        compiler_params=_cparams(2, vmem),
        name="norm_matmul",
    )(x, g.reshape(1, d), w)


def _matmul_residual_kernel(x_ref, a1_ref, a2_ref, w1_ref, w2_ref, o_ref):
    o_ref[...] = x_ref[...] + _dot(a1_ref[...], w1_ref[...]) + _dot(a2_ref[...], w2_ref[...])


def matmul_residual(x, a1, c1, a2, c2, w, *, tm):
    m, n = x.shape
    k1 = w.shape[0] // 2
    vmem = (4 * tm * n * 4 + 4 * tm * k1 * 2 + 4 * k1 * n * 2) / MIB + 8
    return pl.pallas_call(
        _matmul_residual_kernel,
        grid=(m // tm,),
        in_specs=[
            pl.BlockSpec((tm, n), lambda i: (i, 0)),
            pl.BlockSpec((tm, k1), lambda i: (i, c1)),
            pl.BlockSpec((tm, k1), lambda i: (i, c2)),
            pl.BlockSpec((k1, n), lambda i: (0, 0)),
            pl.BlockSpec((k1, n), lambda i: (1, 0)),
        ],
        out_specs=pl.BlockSpec((tm, n), lambda i: (i, 0)),
        out_shape=jax.ShapeDtypeStruct((m, n), F32),
        compiler_params=_cparams(1, vmem),
        name="matmul_residual",
    )(x, a1, a2, w, w)


def _ffn_kernel(x_ref, g_ref, wg_ref, wu_ref, cw_ref, cb_ref, wd_ref, gf_ref, o_ref,
                h_ref, carry_ref, *, tm, seq, row_chunk, final_norm):
    i = pl.program_id(0)
    j = pl.program_id(1)
    nj = pl.num_programs(1)

    @pl.when(j == 0)
    def _():
        _rms_to_ref(x_ref, g_ref, h_ref, row_chunk, copy_ref=o_ref)

    h = h_ref[...]
    g = _dot(h, wg_ref[...])
    u = _dot(h, wu_ref[...])

    seq_start = (i * tm) % seq == 0
    prev = jnp.where(seq_start, 0.0, carry_ref[j])
    carry_ref[j] = g[tm - SUBLANES:, :]
    row = lax.broadcasted_iota(I32, (SUBLANES, g.shape[1]), 0)
    g1 = pltpu.roll(g, 1, 0)
    g2 = pltpu.roll(g, 2, 0)
    g1_top = jnp.where(row == 0, prev[7:8, :], g1[:SUBLANES])
    g2_top = jnp.where(row == 0, prev[6:7, :], jnp.where(row == 1, prev[7:8, :], g2[:SUBLANES]))
    g1 = jnp.concatenate([g1_top, g1[SUBLANES:]], axis=0)
    g2 = jnp.concatenate([g2_top, g2[SUBLANES:]], axis=0)
    cw = cw_ref[...]
    gc = cb_ref[...] + cw[0:1] * g2 + cw[1:2] * g1 + cw[2:3] * g
    act = (gc * jax.nn.sigmoid(gc) * u).astype(BF16)
    o_ref[...] += _dot(act, wd_ref[...])

    if final_norm:
        @pl.when(j == nj - 1)
        def _():
            _rms_to_ref(o_ref, gf_ref, o_ref, row_chunk)


def conv_ffn(x, g, w_gu, conv_w, conv_b, w_down, g_final, *, seq, tm, tf,
             final_norm, row_chunk=128):
    m, d = x.shape
    f = w_down.shape[0]
    nf = f // tf
    vmem = (4 * tm * d * 4 + tm * d * 2 + 4 * d * tf * 2 + 2 * tf * d * 2) / MIB + 8
    return pl.pallas_call(
        functools.partial(_ffn_kernel, tm=tm, seq=seq, row_chunk=row_chunk,
                          final_norm=final_norm),
        grid=(m // tm, nf),
        in_specs=[
            pl.BlockSpec((tm, d), lambda i, j: (i, 0)),
            pl.BlockSpec((1, d), lambda i, j: (0, 0)),
            pl.BlockSpec((d, tf), lambda i, j: (0, j)),
            pl.BlockSpec((d, tf), lambda i, j: (0, j + nf)),
            pl.BlockSpec((FFN_CONV, tf), lambda i, j: (0, j)),
            pl.BlockSpec((1, tf), lambda i, j: (0, j)),
            pl.BlockSpec((tf, d), lambda i, j: (j, 0)),
            pl.BlockSpec((1, d), lambda i, j: (0, 0)),
        ],
        out_specs=pl.BlockSpec((tm, d), lambda i, j: (i, 0)),
        out_shape=jax.ShapeDtypeStruct((m, d), F32),
        scratch_shapes=[pltpu.VMEM((tm, d), BF16),
                        pltpu.VMEM((nf, SUBLANES, tf), F32)],
        compiler_params=_cparams(2, vmem),
        name="conv_ffn",
    )(x, g.reshape(1, d), w_gu, w_gu, conv_w, conv_b.reshape(1, f), w_down,
      g_final.reshape(1, d))


def _retention_kernel(q_ref, k_ref, v_ref, gate_ref, cos_ref, sin_ref, dec_ref, qd_ref,
                      kd_ref, cd_ref, gn_ref, o_ref, state_ref, *, seq, hp):
    c = RET_CHUNK
    cpi = 4
    scale = RET_DIM ** -0.5
    state_ref[...] = jnp.zeros(state_ref.shape, F32)

    def chunk_group(gi, carry):
        sls = [pl.ds(pl.multiple_of((gi * cpi + cc) * c, c), c) for cc in range(cpi)]
        cos = [cos_ref[sl, :] for sl in sls]
        sin = [sin_ref[sl, :] for sl in sls]
        for hh in range(hp):
            cols = slice(hh * RET_DIM, (hh + 1) * RET_DIM)
            state = state_ref[hh]
            for cc, sl in enumerate(sls):
                q = q_ref[0, sl, cols]
                k = k_ref[0, sl, cols]
                v = v_ref[0, sl, cols].astype(BF16)
                qr = q * cos[cc] + pltpu.roll(q, RET_DIM // 2, 1) * sin[cc]
                kr = (k * cos[cc] + pltpu.roll(k, RET_DIM // 2, 1) * sin[cc]) * scale
                scores = _dot_nt(qr.astype(BF16), kr.astype(BF16)) * dec_ref[hh]
                y = _dot(scores.astype(BF16), v)
                y = y + _dot((qr * qd_ref[hh]).astype(BF16), state.astype(BF16))
                kv = _dot_tn((kr * kd_ref[hh]).astype(BF16), v)
                state = cd_ref[hh][0:1, :] * state + kv
                y = y * lax.rsqrt(jnp.mean(y * y, axis=-1, keepdims=True) + EPS)
                y = y * gn_ref[:, cols]
                gate = gate_ref[0, sl, cols]
                o_ref[0, sl, cols] = (y * (gate * jax.nn.sigmoid(gate))).astype(o_ref.dtype)
            state_ref[hh] = state
        return carry

    lax.fori_loop(0, seq // (c * cpi), chunk_group, 0)


def retention(proj, gn, *, seq, hp=4):
    b = proj.shape[0]
    h_ = RET_HEADS
    c = RET_CHUNK
    half = RET_DIM // 2
    freqs = ROPE_BASE ** (-jnp.arange(half, dtype=F32) / half)
    ang = jnp.arange(seq, dtype=F32)[:, None] * freqs[None, :]
    cos2 = jnp.concatenate([jnp.cos(ang), jnp.cos(ang)], axis=-1)
    sin2 = jnp.concatenate([-jnp.sin(ang), jnp.sin(ang)], axis=-1)
    log_g = jnp.log1p(-jnp.exp2(-5.0 - jnp.arange(h_, dtype=F32)))
    pos = jnp.arange(c, dtype=F32)
    diff = pos[:, None] - pos[None, :]
    inner = jnp.where(diff[None] >= 0,
                      jnp.exp(jnp.maximum(diff, 0.0)[None] * log_g[:, None, None]), 0.0)
    q_decay = jnp.exp((pos[None, :] + 1.0) * log_g[:, None])
    k_decay = jnp.exp((c - 1.0 - pos[None, :]) * log_g[:, None])
    chunk_decay = jnp.exp(c * log_g)
    qd = jnp.broadcast_to(q_decay[:, :, None], (h_, c, LANES))
    kd = jnp.broadcast_to(k_decay[:, :, None], (h_, c, LANES))
    cd = jnp.broadcast_to(chunk_decay[:, None, None], (h_, SUBLANES, LANES))

    ng = h_ // hp
    wide = hp * RET_DIM
    head_spec = lambda sec: pl.BlockSpec((1, seq, wide), lambda bi, hi: (bi, 0, sec * ng + hi))
    const_spec = pl.BlockSpec((seq, LANES), lambda bi, hi: (0, 0))
    per_head = lambda r: pl.BlockSpec((hp, r, LANES), lambda bi, hi: (hi, 0, 0))
    vmem = (2 * 4 * seq * wide * 4 + 2 * seq * wide * 2 + 4 * seq * LANES * 4) / MIB + 6
    return pl.pallas_call(
        functools.partial(_retention_kernel, seq=seq, hp=hp),
        grid=(b, ng),
        in_specs=[head_spec(0), head_spec(1), head_spec(2), head_spec(3),
                  const_spec, const_spec, per_head(c), per_head(c), per_head(c),
                  per_head(SUBLANES),
                  pl.BlockSpec((1, wide), lambda bi, hi: (0, hi))],
        out_specs=pl.BlockSpec((1, seq, wide), lambda bi, hi: (bi, 0, hi)),
        out_shape=jax.ShapeDtypeStruct((b, seq, RET_WIDTH), BF16),
        scratch_shapes=[pltpu.VMEM((hp, RET_DIM, RET_DIM), F32)],
        compiler_params=_cparams(2, vmem),
        name="retention",
    )(proj, proj, proj, proj, cos2, sin2, inner, qd, kd, cd, gn.reshape(1, RET_WIDTH))


def _shift_rows(x_ref, d, row8):
    rows = x_ref.shape[1]
    top = jnp.where(row8 >= d, pltpu.roll(x_ref[0, :SUBLANES, :], d, 0), 0.0)
    return jnp.concatenate([top, x_ref[0, SUBLANES - d:rows - d, :]], axis=0)


def _rglru_kernel(x_ref, y_ref, cw_ref, cb_ref, wa_ref, wx_ref, ba_ref, bx_ref, lam_ref,
                  o_ref, ga_ref, gb_ref, carry_ref, *, seq):
    x = x_ref[0]
    row = lax.broadcasted_iota(I32, x.shape, 0)
    cw = cw_ref[...]
    row8 = row[:SUBLANES]
    xc = cb_ref[...] + cw[0:1] * _shift_rows(x_ref, 3, row8)
    xc = xc + cw[1:2] * _shift_rows(x_ref, 2, row8)
    xc = xc + cw[2:3] * _shift_rows(x_ref, 1, row8)
    xc = xc + cw[3:4] * x
    xb = xc.astype(BF16)
    r = jax.nn.sigmoid(_dot(xb, wa_ref[0].astype(BF16)) + ba_ref[...])
    ig = jax.nn.sigmoid(_dot(xb, wx_ref[0].astype(BF16)) + bx_ref[...])
    z = -lam_ref[...]
    softplus = jnp.maximum(z, 0.0) + jnp.log1p(jnp.exp(-jnp.abs(z)))
    log_a = -LRU_C * r * softplus
    a = jnp.exp(log_a)
    v = jnp.maximum(1.0 - a * a, 0.0)
    mult = jnp.where(v > 0.0, v * lax.rsqrt(v), 0.0)
    bb = mult * (ig * xc)

    def doubling(a, bb, idx, limit, axis):
        d = 1
        while d < limit:
            ok = idx >= d
            a_sh = jnp.where(ok, pltpu.roll(a, d, axis), 1.0)
            b_sh = jnp.where(ok, pltpu.roll(bb, d, axis), 0.0)
            bb = bb + a * b_sh
            a = a * a_sh
            d *= 2
        return a, bb

    ng = seq // SUBLANES
    grp = (ng, SUBLANES, x.shape[1])
    sub = lax.broadcasted_iota(I32, grp, 1)
    a, bb = doubling(a.reshape(grp), bb.reshape(grp), sub, SUBLANES, 1)
    a, bb = a.reshape(x.shape), bb.reshape(x.shape)
    ga_ref[...] = a
    gb_ref[...] = bb
    last = pl.ds(SUBLANES - 1, ng, stride=SUBLANES)
    grow = lax.broadcasted_iota(I32, (ng, x.shape[1]), 0)
    _, h_end = doubling(ga_ref[last, :], gb_ref[last, :], grow, ng, 0)
    carry_ref[...] = jnp.where(grow >= 1, pltpu.roll(h_end, 1, 0), 0.0)

    yv = y_ref[0]
    gelu = 0.5 * yv * (1.0 + jnp.tanh(math.sqrt(2.0 / math.pi) * (yv + 0.044715 * (yv * yv * yv))))
    h = jnp.concatenate(
        [bb[g * SUBLANES:(g + 1) * SUBLANES] + a[g * SUBLANES:(g + 1) * SUBLANES] * carry_ref[g:g + 1, :]
         for g in range(ng)], axis=0)
    o_ref[0] = (h * gelu).astype(o_ref.dtype)


def rglru(proj, conv_w, conv_b, wa, ba, wx, bx, lam, *, seq, x_off, y_off):
    b = proj.shape[0]
    nb = LRU_BLOCKS
    w = LRU_WIDTH
    blk = lambda off: pl.BlockSpec((1, seq, LANES), lambda bi, ni: (bi, 0, off + ni))
    vec = pl.BlockSpec((1, LANES), lambda bi, ni: (0, ni))
    mat = pl.BlockSpec((1, LRU_BLOCK_DIM, LRU_BLOCK_DIM), lambda bi, ni: (ni, 0, 0))
    return pl.pallas_call(
        functools.partial(_rglru_kernel, seq=seq),
        grid=(b, nb),
        in_specs=[blk(x_off), blk(y_off),
                  pl.BlockSpec((LRU_CONV, LANES), lambda bi, ni: (0, ni)),
                  vec, mat, mat, vec, vec, vec],
        out_specs=pl.BlockSpec((1, seq, LANES), lambda bi, ni: (bi, 0, ni)),
        out_shape=jax.ShapeDtypeStruct((b, seq, w), BF16),
        scratch_shapes=[pltpu.VMEM((seq, LANES), F32), pltpu.VMEM((seq, LANES), F32),
                        pltpu.VMEM((seq // SUBLANES, LANES), F32)],
        compiler_params=_cparams(2, 48),
        name="rglru",
    )(proj, proj, conv_w, conv_b.reshape(1, w), wa, wx, ba.reshape(1, w), bx.reshape(1, w),
      lam.reshape(1, w))


def _t5_bucket_np(rel):
    n = np.maximum(rel, 0)
    max_exact = REL_BUCKETS // 2
    nf = np.maximum(n, max_exact).astype(np.float64)
    large = max_exact + (np.log(nf / max_exact) / math.log(REL_MAX_DIST / max_exact)
                         * (REL_BUCKETS - max_exact)).astype(np.int32)
    large = np.minimum(large, REL_BUCKETS - 1)
    return np.where(n < max_exact, n, large).astype(np.int32)


def _dsa_kernel(rb_ref, bk_ref, q_ref, k_ref, v_ref, qi_ref, kwa_ref, kwq_ref, o_ref,
                kb_ref, vt_ref, kia_ref, kib_ref, qs_ref, qst_ref, key_ref, hi_ref, lo_ref,
                negm_ref, bias_ref, s_ref, m_ref, acc_ref, pos_ref, *, seq, tq, topk):
    bi = pl.program_id(0)
    i = pl.program_id(1)
    t0 = i * tq
    sc = 2 * LANES
    nsc = seq // sc
    gt = GROUP * tq
    nskip = topk // tq
    log2e = math.log2(math.e)
    att_scale = ATT_DIM ** -0.5 * log2e
    idx_scale = IDX_HEADS ** -0.5 * IDX_DIM ** -0.5
    nsel = (t0 + tq + sc - 1) // sc

    @pl.when(i == 0)
    def _():
        lane = lax.broadcasted_iota(I32, (sc, LANES), 1)
        for u in range(nsc):
            kblk = k_ref[0, u * sc:(u + 1) * sc, :]
            vblk = v_ref[0, u * sc:(u + 1) * sc, :]
            for kh in range(KV_HEADS):
                kb_ref[kh, u] = kblk[:, kh * ATT_DIM:(kh + 1) * ATT_DIM].astype(BF16)
                vt_ref[kh, u] = vblk[:, kh * ATT_DIM:(kh + 1) * ATT_DIM].astype(F32).T.astype(BF16)
            kw = kwa_ref[0, u * sc:(u + 1) * sc, :].astype(F32)
            ka = jnp.where(lane < IDX_DIM, kw, 0.0)
            kia_ref[u] = ka.astype(BF16)
            kib_ref[u] = pltpu.roll(ka, IDX_DIM, 1).astype(BF16)

    @pl.when((bi == 0) & (i == 0))
    def _():
        bias_ref[...] = jnp.zeros(bias_ref.shape, F32)

        def fill(bu, c):
            for r in range(2):
                hit = bk_ref[r] == bu
                for h in range(ATT_HEADS):
                    kh, g = divmod(h, GROUP)
                    val = (rb_ref[bu, h] - rb_ref[REL_BUCKETS - 1, h]) * log2e
                    rows = slice((2 + r) * LANES, (3 + r) * LANES)
                    cur = bias_ref[kh, rows, g * tq:(g + 1) * tq]
                    bias_ref[kh, rows, g * tq:(g + 1) * tq] = jnp.where(hit, val, cur)
            return c

        lax.fori_loop(0, REL_BUCKETS, fill, 0)

    for h in range(ATT_HEADS):
        kh, g = divmod(h, GROUP)
        qh = q_ref[0, :, h * ATT_DIM:(h + 1) * ATT_DIM].astype(BF16)
        qs_ref[kh, :, g * tq:(g + 1) * tq] = qh.T

    row_s = lax.broadcasted_iota(I32, (sc, tq), 0)
    tidx = t0 + lax.broadcasted_iota(I32, (sc, tq), 1)

    @pl.when(i < nskip)
    def _():
        for u in range((nskip * tq + sc - 1) // sc):
            negm_ref[u] = jnp.where(u * sc + row_s <= tidx, 0.0, NEG_INF)

    @pl.when(i >= nskip)
    def _():
        for pp in range(IDX_HEADS // 4):
            for r in range(2):
                cols = slice((2 * pp + r) * LANES, (2 * pp + r + 1) * LANES)
                qst_ref[pp, :, r * tq:(r + 1) * tq] = qi_ref[0, :, cols].astype(BF16).T
        wt = kwq_ref[0].T * idx_scale

        def head_w(h):
            return wt[IDX_DIM + h:IDX_DIM + h + 1, :]

        def score_unit(u, carry):
            ka = kia_ref[u]
            kb_ = kib_ref[u]
            acc = jnp.zeros((sc, tq), F32)
            for pp in range(IDX_HEADS // 4):
                q2 = qst_ref[pp]
                sa = jnp.maximum(_dot(ka, q2), 0.0)
                sb = jnp.maximum(_dot(kb_, q2), 0.0)
                acc = acc + sa[:, :tq] * head_w(4 * pp) + sb[:, :tq] * head_w(4 * pp + 1)
                acc = acc + sa[:, tq:] * head_w(4 * pp + 2) + sb[:, tq:] * head_w(4 * pp + 3)
            score = jnp.where(u * sc + row_s <= tidx, acc + 0.0, NEG_INF)
            bits = lax.bitcast_convert_type(score, I32)
            key = jnp.where(bits < 0, bits ^ jnp.int32(0x7FFFFFFF), bits)
            key_ref[u] = key
            hi_ref[u] = lax.shift_right_arithmetic(key, 16).astype(I16)
            lo_ref[u] = ((key & 0xFFFF) - HALF16).astype(I16)
            return carry

        lax.fori_loop(0, nsel, score_unit, 0)

        @pl.when(nsel % 2 == 1)
        def _():
            hi_ref[nsel] = jnp.full((sc, tq), -HALF16, I16)
            lo_ref[nsel] = jnp.full((sc, tq), -HALF16, I16)

        npair = (nsel + 1) // 2
        rows16 = 2 * SUBLANES
        one16, zero16 = jnp.int16(1), jnp.int16(0)

        def count16(ref, cand):
            c16 = jnp.broadcast_to(cand.astype(I16), (rows16, tq))

            def body(pi, accs):
                accs = list(accs)
                for w in range(2):
                    kk = ref[2 * pi + w]
                    for r in range(sc // rows16):
                        hit = jnp.where(kk[r * rows16:(r + 1) * rows16] >= c16, one16, zero16)
                        accs[r % len(accs)] = accs[r % len(accs)] + hit
                return tuple(accs)

            accs = lax.fori_loop(0, npair, body,
                                 tuple(jnp.zeros((rows16, tq), I16) for _ in range(4)))
            tot = (accs[0] + accs[1]) + (accs[2] + accs[3])
            return tot.astype(I32).sum(axis=0, keepdims=True)

        def search16(ref):
            def step(it, thr):
                cand = thr + lax.shift_left(jnp.int32(1), 15 - it)
                return jnp.where(count16(ref, cand) >= topk, cand, thr)
            return lax.fori_loop(0, 16, step, jnp.full((1, tq), -HALF16, I32))

        t_hi = search16(hi_ref)
        t_hi16 = jnp.broadcast_to(t_hi.astype(I16), (rows16, tq))

        def fold(pi, carry):
            for w in range(2):
                u = 2 * pi + w
                for r in range(sc // rows16):
                    rs = slice(r * rows16, (r + 1) * rows16)
                    hi = hi_ref[u, rs, :]
                    lo_ref[u, rs, :] = jnp.where(
                        hi > t_hi16, jnp.int16(HALF16 - 1),
                        jnp.where(hi == t_hi16, lo_ref[u, rs, :], jnp.int16(-HALF16)))
            return carry

        lax.fori_loop(0, npair, fold, 0)
        t_lo = search16(lo_ref)
        thr = t_hi * (2 * HALF16) + (t_lo + HALF16)

        def count(pred):
            def body(c2, acc8):
                hit = pred(key_ref[c2], c2 * sc + row_s).astype(I32)
                return acc8 + hit.reshape(sc // SUBLANES, SUBLANES, tq).sum(axis=0)
            acc8 = lax.fori_loop(0, nsel, body, jnp.zeros((SUBLANES, tq), I32))
            return acc8.sum(axis=0, keepdims=True)

        need = topk - count(lambda kk, sidx: kk > thr)
        n_tie = count(lambda kk, sidx: kk == thr)

        nbits = int(seq).bit_length()
        pos_ref[...] = jnp.full((1, tq), 2 ** nbits, I32)

        @pl.when(jnp.max(jnp.where(n_tie != need, 1, 0)) > 0)
        def _():
            def tie_step(it, pos):
                cand = pos + lax.shift_left(jnp.int32(1), nbits - 1 - it)
                cnt = count(lambda kk, sidx: (kk == thr) & (sidx < cand))
                return jnp.where(cnt <= need, cand, pos)

            pos_ref[...] = lax.fori_loop(0, nbits, tie_step, jnp.zeros((1, tq), I32))

        pos = pos_ref[...]

        def mask_unit(u, carry):
            kk = key_ref[u]
            sidx = u * sc + row_s
            sel = (kk > thr) | ((kk == thr) & (sidx < pos))
            negm_ref[u] = jnp.where(sel & (sidx <= tidx), 0.0, NEG_INF)
            return carry

        lax.fori_loop(0, nsel, mask_unit, 0)

    nfar = jnp.maximum((i - 1) // 2, 0)
    part = (sc // SUBLANES, SUBLANES, gt)

    def logits_unit(u, mparts, with_bias):
        nm = negm_ref[u]
        nm = jnp.concatenate([nm] * GROUP, axis=1)
        out = []
        for kh in range(KV_HEADS):
            s = _dot(kb_ref[kh, u], qs_ref[kh]) * att_scale + nm
            if with_bias:
                start = pl.multiple_of((2 * u - i + 3) * LANES, LANES)
                s = s + bias_ref[kh, pl.ds(start, sc), :]
            s_ref[kh, u] = s
            out.append(jnp.maximum(mparts[kh], s.reshape(part).max(axis=0)))
        return tuple(out)

    mparts = tuple(jnp.full((SUBLANES, gt), -jnp.inf, F32) for _ in range(KV_HEADS))
    mparts = lax.fori_loop(0, nfar, lambda u, c: logits_unit(u, c, False), mparts)
    mparts = lax.fori_loop(nfar, nsel, lambda u, c: logits_unit(u, c, True), mparts)
    for kh in range(KV_HEADS):
        m_ref[kh] = jnp.broadcast_to(mparts[kh].max(axis=0, keepdims=True), (SUBLANES, gt))
    acc_ref[...] = jnp.zeros(acc_ref.shape, F32)

    def probs_units(u0, nu, lparts):
        out = []
        for kh in range(KV_HEADS):
            lp = lparts[kh]
            ps = []
            for w in range(nu):
                p = jnp.exp2(s_ref[kh, u0 + w].reshape(part) - m_ref[kh][None])
                lp = lp + p.sum(axis=0)
                ps.append(p.reshape(sc, gt).astype(BF16))
            vt = jnp.concatenate([vt_ref[kh, u0 + w] for w in range(nu)], axis=1)
            acc_ref[kh] += _dot(vt, jnp.concatenate(ps, axis=0))
            out.append(lp)
        return tuple(out)

    lparts = tuple(jnp.zeros((SUBLANES, gt), F32) for _ in range(KV_HEADS))
    lparts = lax.fori_loop(0, nsel // 2, lambda pi, c: probs_units(2 * pi, 2, c), lparts)
    lparts = lax.cond(nsel % 2 == 1, lambda c: probs_units(nsel - 1, 1, c), lambda c: c, lparts)
    for kh in range(KV_HEADS):
        out_t = acc_ref[kh] / lparts[kh].sum(axis=0, keepdims=True)
        for g in range(GROUP):
            h = kh * GROUP + g
            o_ref[0, :, h * ATT_DIM:(h + 1) * ATT_DIM] = (
                out_t[:, g * tq:(g + 1) * tq].astype(o_ref.dtype).T)


def dsa_attention(proj, kw_f32, rel_bias, *, seq, tq=LANES):
    assert tq == LANES
    b = proj.shape[0]
    topk = min(TOPK_MAX, seq // 4)
    assert topk % tq == 0 and seq % (2 * LANES) == 0
    o_q = ATT_HEADS * ATT_DIM
    o_kv = KV_HEADS * ATT_DIM
    o_qi = IDX_HEADS * IDX_DIM
    sc = 2 * LANES
    nsc = seq // sc
    gt = GROUP * tq
    jj, ii = np.meshgrid(np.arange(tq), np.arange(tq), indexing="ij")
    buckets = jnp.asarray(np.stack([_t5_bucket_np(tq + ii - jj), _t5_bucket_np(ii - jj)]))
    once = pl.Buffered(1)
    in_specs = [
        pl.BlockSpec(memory_space=pltpu.SMEM),
        pl.BlockSpec((2, tq, tq), lambda bi, i: (0, 0, 0)),
        pl.BlockSpec((1, tq, o_q), lambda bi, i: (bi, i, 0)),
        pl.BlockSpec((1, seq, o_kv), lambda bi, i: (bi, 0, o_q // o_kv), pipeline_mode=once),
        pl.BlockSpec((1, seq, o_kv), lambda bi, i: (bi, 0, o_q // o_kv + 1), pipeline_mode=once),
        pl.BlockSpec((1, tq, o_qi), lambda bi, i: (bi, i, (o_q + 2 * o_kv) // o_qi)),
        pl.BlockSpec((1, seq, LANES), lambda bi, i: (bi, 0, (o_q + 2 * o_kv + o_qi) // LANES),
                     pipeline_mode=once),
        pl.BlockSpec((1, tq, LANES), lambda bi, i: (bi, i, 0)),
    ]
    scratch = [
        pltpu.VMEM((KV_HEADS, nsc, sc, ATT_DIM), BF16),
        pltpu.VMEM((KV_HEADS, nsc, ATT_DIM, sc), BF16),
        pltpu.VMEM((nsc, sc, LANES), BF16),
        pltpu.VMEM((nsc, sc, LANES), BF16),
        pltpu.VMEM((KV_HEADS, ATT_DIM, gt), BF16),
        pltpu.VMEM((IDX_HEADS // 4, LANES, 2 * tq), BF16),
        pltpu.VMEM((nsc, sc, tq), I32),
        pltpu.VMEM((nsc, sc, tq), I16),
        pltpu.VMEM((nsc, sc, tq), I16),
        pltpu.VMEM((nsc, sc, tq), F32),
        pltpu.VMEM((KV_HEADS, 5 * LANES, gt), F32),
        pltpu.VMEM((KV_HEADS, nsc, sc, gt), F32),
        pltpu.VMEM((KV_HEADS, SUBLANES, gt), F32),
        pltpu.VMEM((KV_HEADS, ATT_DIM, gt), F32),
        pltpu.VMEM((1, tq), I32),
    ]
    return pl.pallas_call(
        functools.partial(_dsa_kernel, seq=seq, tq=tq, topk=topk),
        grid=(b, seq // tq),
        in_specs=in_specs,
        out_specs=pl.BlockSpec((1, tq, o_q), lambda bi, i: (bi, i, 0)),
        out_shape=jax.ShapeDtypeStruct((b, seq, o_q), BF16),
        scratch_shapes=scratch,
        compiler_params=_cparams(2, 56),
        name="dsa_attention",
    )(rel_bias, buckets, proj, proj, proj, proj, proj, kw_f32)


def kernel(x, norm_mix, norm_ffn, e_w_in, e_ret_gn, e_conv_w, e_conv_b, e_gate_a_w, e_gate_a_b,
           e_gate_x_w, e_gate_x_b, e_lambda, e_w_out, o_w_in, o_w_out, rel_bias,
           ffn_w_gu, ffn_conv_w, ffn_conv_b, ffn_w_down, final_norm):
    b, s, d = x.shape
    m = b * s
    x2 = x.reshape(m, d)

    e_in = e_w_in.shape[-1]
    proj = norm_matmul(x2, norm_mix[0], cast_layer(e_w_in, 0), tm=PROJ_TM, tn=PROJ_TN)
    proj = proj.reshape(b, s, e_in)
    ret = retention(proj, e_ret_gn[0], seq=s)
    lru = rglru(proj, e_conv_w[0], e_conv_b[0], e_gate_a_w[0], e_gate_a_b[0], e_gate_x_w[0],
                e_gate_x_b[0], e_lambda[0], seq=s,
                x_off=4 * RET_HEADS, y_off=4 * RET_HEADS + LRU_BLOCKS)
    x2 = matmul_residual(x2, ret.reshape(m, RET_WIDTH), 0, lru.reshape(m, LRU_WIDTH), 0,
                         cast_layer(e_w_out, 0), tm=OUT_TM)
    x2 = conv_ffn(x2, norm_ffn[0], cast_layer(ffn_w_gu, 0), ffn_conv_w[0], ffn_conv_b[0],
                  cast_layer(ffn_w_down, 0), final_norm, seq=s, tm=FFN_TM, tf=FFN_TF,
                  final_norm=False)

    o_in = o_w_in.shape[-1]
    n_pad = -o_in % PROJ_TN
    w_in = jnp.pad(o_w_in[0].astype(BF16), ((0, 0), (0, n_pad)))
    kw_col = ATT_HEADS * ATT_DIM + 2 * KV_HEADS * ATT_DIM + IDX_HEADS * IDX_DIM
    proj, kw_f32 = norm_matmul(x2, norm_mix[1], w_in, tm=PROJ_TM, tn=PROJ_TN, out_dtype=BF16,
                               side_col=kw_col)
    attn = dsa_attention(proj.reshape(b, s, o_in + n_pad), kw_f32.reshape(b, s, LANES),
                         rel_bias, seq=s)
    attn = attn.reshape(m, ATT_HEADS * ATT_DIM)
    x2 = matmul_residual(x2, attn, 0, attn, 1, cast_layer(o_w_out, 0), tm=OUT_TM)
    x2 = conv_ffn(x2, norm_ffn[1], cast_layer(ffn_w_gu, 1), ffn_conv_w[1], ffn_conv_b[1],
                  cast_layer(ffn_w_down, 1), final_norm, seq=s, tm=FFN_TM, tf=FFN_TF,
                  final_norm=True)
    return x2.reshape(b, s, d)
```

```python
import functools
import math

import numpy as np
import jax
import jax.numpy as jnp
from jax import lax
from jax.experimental import pallas as pl
from jax.experimental.pallas import tpu as pltpu

F32 = jnp.float32
BF16 = jnp.bfloat16
I32 = jnp.int32
I16 = jnp.int16

EPS = 1e-6
NEG_INF = -1e30
ROPE_BASE = 10000.0

RET_HEADS = 8
RET_DIM = 128
RET_WIDTH = RET_HEADS * RET_DIM
RET_CHUNK = 128
LRU_BLOCKS = 8
LRU_BLOCK_DIM = 128
LRU_WIDTH = LRU_BLOCKS * LRU_BLOCK_DIM
LRU_CONV = 4
LRU_C = 8.0

ATT_HEADS = 16
ATT_DIM = 128
KV_HEADS = 4
GROUP = ATT_HEADS // KV_HEADS
IDX_HEADS = 16
IDX_DIM = 64
TOPK_MAX = 256
REL_BUCKETS = 32
REL_MAX_DIST = 128
FFN_CONV = 3

LANES = 128
SUBLANES = 8
MIB = 2 ** 20
HALF16 = 2 ** 15

PROJ_TM, PROJ_TN = 1024, 1536
CAST_BLOCK_BYTES = 6 * MIB
OUT_TM = 512
FFN_TM, FFN_TF = 1024, 512


def _cparams(n_axes, vmem_mib):
    return pltpu.CompilerParams(
        dimension_semantics=("arbitrary",) * n_axes,
        vmem_limit_bytes=int(vmem_mib * MIB))


def _dot(a, b):
    return jnp.dot(a, b, preferred_element_type=F32)


def _dot_nt(a, b):
    return lax.dot_general(a, b, (((1,), (1,)), ((), ())), preferred_element_type=F32)


def _dot_tn(a, b):
    return lax.dot_general(a, b, (((0,), (0,)), ((), ())), preferred_element_type=F32)


def _rms_rows(x, g):
    ms = jnp.mean(x * x, axis=-1, keepdims=True)
    return x * lax.rsqrt(ms + EPS) * g


def _rms_to_ref(x_ref, g_ref, h_ref, row_chunk, copy_ref=None):
    g = g_ref[...]

    def body(r, c):
        sl = pl.ds(pl.multiple_of(r * row_chunk, row_chunk), row_chunk)
        x = x_ref[sl, :]
        h_ref[sl, :] = _rms_rows(x, g).astype(h_ref.dtype)
        if copy_ref is not None:
            copy_ref[sl, :] = x
        return c

    lax.fori_loop(0, x_ref.shape[0] // row_chunk, body, 0)


def _cast_kernel(x_ref, o_ref):
    o_ref[...] = x_ref[...].astype(o_ref.dtype)


def cast_layer(w, layer):
    _, r, c = w.shape
    tr = max(CAST_BLOCK_BYTES // (c * 4) // 16 * 16, 16)
    while r % tr:
        tr -= 16
    vmem = 2 * tr * c * (4 + 2) / MIB + 4
    return pl.pallas_call(
        _cast_kernel,
        grid=(r // tr,),
        in_specs=[pl.BlockSpec((None, tr, c), lambda i: (layer, i, 0))],
        out_specs=pl.BlockSpec((tr, c), lambda i: (i, 0)),
        out_shape=jax.ShapeDtypeStruct((r, c), BF16),
        compiler_params=_cparams(1, vmem),
        name="cast_layer",
    )(w)


def _norm_matmul_kernel(x_ref, g_ref, w_ref, o_ref, *rest, row_chunk, side_col):
    h_ref = rest[-1]
    j = pl.program_id(1)

    @pl.when(j == 0)
    def _():
        _rms_to_ref(x_ref, g_ref, h_ref, row_chunk)

    res = _dot(h_ref[...], w_ref[...])
    o_ref[...] = res.astype(o_ref.dtype)
    if side_col is not None:
        side_ref = rest[0]
        tn = w_ref.shape[1]

        @pl.when(j == side_col // tn)
        def _():
            off = side_col % tn
            side_ref[...] = res[:, off:off + LANES]


def norm_matmul(x, g, w, *, tm, tn, out_dtype=F32, side_col=None, row_chunk=128):
    m, d = x.shape
    n = w.shape[1]
    out_bytes = jnp.dtype(out_dtype).itemsize
    vmem = (2 * tm * d * 4 + 2 * d * tn * 2 + 2 * tm * tn * out_bytes + tm * d * 2) / MIB + 8
    out_specs = pl.BlockSpec((tm, tn), lambda i, j: (i, j))
    out_shape = jax.ShapeDtypeStruct((m, n), out_dtype)
    if side_col is not None:
        out_specs = [out_specs, pl.BlockSpec((tm, LANES), lambda i, j: (i, 0))]
        out_shape = [out_shape, jax.ShapeDtypeStruct((m, LANES), F32)]
    return pl.pallas_call(
        functools.partial(_norm_matmul_kernel, row_chunk=row_chunk, side_col=side_col),
        grid=(m // tm, n // tn),
        in_specs=[
            pl.BlockSpec((tm, d), lambda i, j: (i, 0)),
            pl.BlockSpec((1, d), lambda i, j: (0, 0)),
            pl.BlockSpec((d, tn), lambda i, j: (0, j)),
        ],
        out_specs=out_specs,
        out_shape=out_shape,
        scratch_shapes=[pltpu.VMEM((tm, d), BF16)],
        compiler_params=_cparams(2, vmem),
        name="norm_matmul",
    )(x, g.reshape(1, d), w)


def _matmul_residual_kernel(x_ref, a1_ref, a2_ref, w1_ref, w2_ref, o_ref):
    o_ref[...] = x_ref[...] + _dot(a1_ref[...], w1_ref[...]) + _dot(a2_ref[...], w2_ref[...])


def matmul_residual(x, a1, c1, a2, c2, w, *, tm):
    m, n = x.shape
    k1 = w.shape[0] // 2
    vmem = (4 * tm * n * 4 + 4 * tm * k1 * 2 + 4 * k1 * n * 2) / MIB + 8
    return pl.pallas_call(
        _matmul_residual_kernel,
        grid=(m // tm,),
        in_specs=[
            pl.BlockSpec((tm, n), lambda i: (i, 0)),
            pl.BlockSpec((tm, k1), lambda i: (i, c1)),
            pl.BlockSpec((tm, k1), lambda i: (i, c2)),
            pl.BlockSpec((k1, n), lambda i: (0, 0)),
            pl.BlockSpec((k1, n), lambda i: (1, 0)),
        ],
        out_specs=pl.BlockSpec((tm, n), lambda i: (i, 0)),
        out_shape=jax.ShapeDtypeStruct((m, n), F32),
        compiler_params=_cparams(1, vmem),
        name="matmul_residual",
    )(x, a1, a2, w, w)


def _ffn_kernel(x_ref, g_ref, wg_ref, wu_ref, cw_ref, cb_ref, wd_ref, gf_ref, o_ref,
                h_ref, carry_ref, *, tm, seq, row_chunk, final_norm):
    i = pl.program_id(0)
    j = pl.program_id(1)
    nj = pl.num_programs(1)

    @pl.when(j == 0)
    def _():
        _rms_to_ref(x_ref, g_ref, h_ref, row_chunk, copy_ref=o_ref)

    h = h_ref[...]
    g = _dot(h, wg_ref[...])
    u = _dot(h, wu_ref[...])

    seq_start = (i * tm) % seq == 0
    prev = jnp.where(seq_start, 0.0, carry_ref[j])
    carry_ref[j] = g[tm - SUBLANES:, :]
    row = lax.broadcasted_iota(I32, (SUBLANES, g.shape[1]), 0)
    g1 = pltpu.roll(g, 1, 0)
    g2 = pltpu.roll(g, 2, 0)
    g1_top = jnp.where(row == 0, prev[7:8, :], g1[:SUBLANES])
    g2_top = jnp.where(row == 0, prev[6:7, :], jnp.where(row == 1, prev[7:8, :], g2[:SUBLANES]))
    g1 = jnp.concatenate([g1_top, g1[SUBLANES:]], axis=0)
    g2 = jnp.concatenate([g2_top, g2[SUBLANES:]], axis=0)
    cw = cw_ref[...]
    gc = cb_ref[...] + cw[0:1] * g2 + cw[1:2] * g1 + cw[2:3] * g
    act = (gc * jax.nn.sigmoid(gc) * u).astype(BF16)
    o_ref[...] += _dot(act, wd_ref[...])

    if final_norm:
        @pl.when(j == nj - 1)
        def _():
            _rms_to_ref(o_ref, gf_ref, o_ref, row_chunk)


def conv_ffn(x, g, w_gu, conv_w, conv_b, w_down, g_final, *, seq, tm, tf,
             final_norm, row_chunk=128):
    m, d = x.shape
    f = w_down.shape[0]
    nf = f // tf
    vmem = (4 * tm * d * 4 + tm * d * 2 + 4 * d * tf * 2 + 2 * tf * d * 2) / MIB + 8
    return pl.pallas_call(
        functools.partial(_ffn_kernel, tm=tm, seq=seq, row_chunk=row_chunk,
                          final_norm=final_norm),
        grid=(m // tm, nf),
        in_specs=[
            pl.BlockSpec((tm, d), lambda i, j: (i, 0)),
            pl.BlockSpec((1, d), lambda i, j: (0, 0)),
            pl.BlockSpec((d, tf), lambda i, j: (0, j)),
            pl.BlockSpec((d, tf), lambda i, j: (0, j + nf)),
            pl.BlockSpec((FFN_CONV, tf), lambda i, j: (0, j)),
            pl.BlockSpec((1, tf), lambda i, j: (0, j)),
            pl.BlockSpec((tf, d), lambda i, j: (j, 0)),
            pl.BlockSpec((1, d), lambda i, j: (0, 0)),
        ],
        out_specs=pl.BlockSpec((tm, d), lambda i, j: (i, 0)),
        out_shape=jax.ShapeDtypeStruct((m, d), F32),
        scratch_shapes=[pltpu.VMEM((tm, d), BF16),
                        pltpu.VMEM((nf, SUBLANES, tf), F32)],
        compiler_params=_cparams(2, vmem),
        name="conv_ffn",
    )(x, g.reshape(1, d), w_gu, w_gu, conv_w, conv_b.reshape(1, f), w_down,
      g_final.reshape(1, d))


def _retention_kernel(q_ref, k_ref, v_ref, gate_ref, cos_ref, sin_ref, dec_ref, qd_ref,
                      kd_ref, cd_ref, gn_ref, o_ref, state_ref, *, seq, hp):
    c = RET_CHUNK
    cpi = 4
    scale = RET_DIM ** -0.5
    state_ref[...] = jnp.zeros(state_ref.shape, F32)

    def chunk_group(gi, carry):
        sls = [pl.ds(pl.multiple_of((gi * cpi + cc) * c, c), c) for cc in range(cpi)]
        cos = [cos_ref[sl, :] for sl in sls]
        sin = [sin_ref[sl, :] for sl in sls]
        for hh in range(hp):
            cols = slice(hh * RET_DIM, (hh + 1) * RET_DIM)
            state = state_ref[hh]
            for cc, sl in enumerate(sls):
                q = q_ref[0, sl, cols]
                k = k_ref[0, sl, cols]
                v = v_ref[0, sl, cols].astype(BF16)
                qr = q * cos[cc] + pltpu.roll(q, RET_DIM // 2, 1) * sin[cc]
                kr = (k * cos[cc] + pltpu.roll(k, RET_DIM // 2, 1) * sin[cc]) * scale
                scores = _dot_nt(qr.astype(BF16), kr.astype(BF16)) * dec_ref[hh]
                y = _dot(scores.astype(BF16), v)
                y = y + _dot((qr * qd_ref[hh]).astype(BF16), state.astype(BF16))
                kv = _dot_tn((kr * kd_ref[hh]).astype(BF16), v)
                state = cd_ref[hh][0:1, :] * state + kv
                y = y * lax.rsqrt(jnp.mean(y * y, axis=-1, keepdims=True) + EPS)
                y = y * gn_ref[:, cols]
                gate = gate_ref[0, sl, cols]
                o_ref[0, sl, cols] = (y * (gate * jax.nn.sigmoid(gate))).astype(o_ref.dtype)
            state_ref[hh] = state
        return carry

    lax.fori_loop(0, seq // (c * cpi), chunk_group, 0)


def retention(proj, gn, *, seq, hp=4):
    b = proj.shape[0]
    h_ = RET_HEADS
    c = RET_CHUNK
    half = RET_DIM // 2
    freqs = ROPE_BASE ** (-jnp.arange(half, dtype=F32) / half)
    ang = jnp.arange(seq, dtype=F32)[:, None] * freqs[None, :]
    cos2 = jnp.concatenate([jnp.cos(ang), jnp.cos(ang)], axis=-1)
    sin2 = jnp.concatenate([-jnp.sin(ang), jnp.sin(ang)], axis=-1)
    log_g = jnp.log1p(-jnp.exp2(-5.0 - jnp.arange(h_, dtype=F32)))
    pos = jnp.arange(c, dtype=F32)
    diff = pos[:, None] - pos[None, :]
    inner = jnp.where(diff[None] >= 0,
                      jnp.exp(jnp.maximum(diff, 0.0)[None] * log_g[:, None, None]), 0.0)
    q_decay = jnp.exp((pos[None, :] + 1.0) * log_g[:, None])
    k_decay = jnp.exp((c - 1.0 - pos[None, :]) * log_g[:, None])
    chunk_decay = jnp.exp(c * log_g)
    qd = jnp.broadcast_to(q_decay[:, :, None], (h_, c, LANES))
    kd = jnp.broadcast_to(k_decay[:, :, None], (h_, c, LANES))
    cd = jnp.broadcast_to(chunk_decay[:, None, None], (h_, SUBLANES, LANES))

    ng = h_ // hp
    wide = hp * RET_DIM
    head_spec = lambda sec: pl.BlockSpec((1, seq, wide), lambda bi, hi: (bi, 0, sec * ng + hi))
    const_spec = pl.BlockSpec((seq, LANES), lambda bi, hi: (0, 0))
    per_head = lambda r: pl.BlockSpec((hp, r, LANES), lambda bi, hi: (hi, 0, 0))
    vmem = (2 * 4 * seq * wide * 4 + 2 * seq * wide * 2 + 4 * seq * LANES * 4) / MIB + 6
    return pl.pallas_call(
        functools.partial(_retention_kernel, seq=seq, hp=hp),
        grid=(b, ng),
        in_specs=[head_spec(0), head_spec(1), head_spec(2), head_spec(3),
                  const_spec, const_spec, per_head(c), per_head(c), per_head(c),
                  per_head(SUBLANES),
                  pl.BlockSpec((1, wide), lambda bi, hi: (0, hi))],
        out_specs=pl.BlockSpec((1, seq, wide), lambda bi, hi: (bi, 0, hi)),
        out_shape=jax.ShapeDtypeStruct((b, seq, RET_WIDTH), BF16),
        scratch_shapes=[pltpu.VMEM((hp, RET_DIM, RET_DIM), F32)],
        compiler_params=_cparams(2, vmem),
        name="retention",
    )(proj, proj, proj, proj, cos2, sin2, inner, qd, kd, cd, gn.reshape(1, RET_WIDTH))


def _shift_rows(x_ref, d, row8):
    rows = x_ref.shape[1]
    top = jnp.where(row8 >= d, pltpu.roll(x_ref[0, :SUBLANES, :], d, 0), 0.0)
    return jnp.concatenate([top, x_ref[0, SUBLANES - d:rows - d, :]], axis=0)


def _rglru_kernel(x_ref, y_ref, cw_ref, cb_ref, wa_ref, wx_ref, ba_ref, bx_ref, lam_ref,
                  o_ref, ga_ref, gb_ref, carry_ref, *, seq):
    x = x_ref[0]
    row = lax.broadcasted_iota(I32, x.shape, 0)
    cw = cw_ref[...]
    row8 = row[:SUBLANES]
    xc = cb_ref[...] + cw[0:1] * _shift_rows(x_ref, 3, row8)
    xc = xc + cw[1:2] * _shift_rows(x_ref, 2, row8)
    xc = xc + cw[2:3] * _shift_rows(x_ref, 1, row8)
    xc = xc + cw[3:4] * x
    xb = xc.astype(BF16)
    r = jax.nn.sigmoid(_dot(xb, wa_ref[0].astype(BF16)) + ba_ref[...])
    ig = jax.nn.sigmoid(_dot(xb, wx_ref[0].astype(BF16)) + bx_ref[...])
    z = -lam_ref[...]
    softplus = jnp.maximum(z, 0.0) + jnp.log1p(jnp.exp(-jnp.abs(z)))
    log_a = -LRU_C * r * softplus
    a = jnp.exp(log_a)
    v = jnp.maximum(1.0 - a * a, 0.0)
    mult = jnp.where(v > 0.0, v * lax.rsqrt(v), 0.0)
    bb = mult * (ig * xc)

    def doubling(a, bb, idx, limit, axis):
        d = 1
        while d < limit:
            ok = idx >= d
            a_sh = jnp.where(ok, pltpu.roll(a, d, axis), 1.0)
            b_sh = jnp.where(ok, pltpu.roll(bb, d, axis), 0.0)
            bb = bb + a * b_sh
            a = a * a_sh
            d *= 2
        return a, bb

    ng = seq // SUBLANES
    grp = (ng, SUBLANES, x.shape[1])
    sub = lax.broadcasted_iota(I32, grp, 1)
    a, bb = doubling(a.reshape(grp), bb.reshape(grp), sub, SUBLANES, 1)
    a, bb = a.reshape(x.shape), bb.reshape(x.shape)
    ga_ref[...] = a
    gb_ref[...] = bb
    last = pl.ds(SUBLANES - 1, ng, stride=SUBLANES)
    grow = lax.broadcasted_iota(I32, (ng, x.shape[1]), 0)
    _, h_end = doubling(ga_ref[last, :], gb_ref[last, :], grow, ng, 0)
    carry_ref[...] = jnp.where(grow >= 1, pltpu.roll(h_end, 1, 0), 0.0)

    yv = y_ref[0]
    gelu = 0.5 * yv * (1.0 + jnp.tanh(math.sqrt(2.0 / math.pi) * (yv + 0.044715 * (yv * yv * yv))))
    h = jnp.concatenate(
        [bb[g * SUBLANES:(g + 1) * SUBLANES] + a[g * SUBLANES:(g + 1) * SUBLANES] * carry_ref[g:g + 1, :]
         for g in range(ng)], axis=0)
    o_ref[0] = (h * gelu).astype(o_ref.dtype)


def rglru(proj, conv_w, conv_b, wa, ba, wx, bx, lam, *, seq, x_off, y_off):
    b = proj.shape[0]
    nb = LRU_BLOCKS
    w = LRU_WIDTH
    blk = lambda off: pl.BlockSpec((1, seq, LANES), lambda bi, ni: (bi, 0, off + ni))
    vec = pl.BlockSpec((1, LANES), lambda bi, ni: (0, ni))
    mat = pl.BlockSpec((1, LRU_BLOCK_DIM, LRU_BLOCK_DIM), lambda bi, ni: (ni, 0, 0))
    return pl.pallas_call(
        functools.partial(_rglru_kernel, seq=seq),
        grid=(b, nb),
        in_specs=[blk(x_off), blk(y_off),
                  pl.BlockSpec((LRU_CONV, LANES), lambda bi, ni: (0, ni)),
                  vec, mat, mat, vec, vec, vec],
        out_specs=pl.BlockSpec((1, seq, LANES), lambda bi, ni: (bi, 0, ni)),
        out_shape=jax.ShapeDtypeStruct((b, seq, w), BF16),
        scratch_shapes=[pltpu.VMEM((seq, LANES), F32), pltpu.VMEM((seq, LANES), F32),
                        pltpu.VMEM((seq // SUBLANES, LANES), F32)],
        compiler_params=_cparams(2, 48),
        name="rglru",
    )(proj, proj, conv_w, conv_b.reshape(1, w), wa, wx, ba.reshape(1, w), bx.reshape(1, w),
      lam.reshape(1, w))


def _t5_bucket_np(rel):
    n = np.maximum(rel, 0)
    max_exact = REL_BUCKETS // 2
    nf = np.maximum(n, max_exact).astype(np.float64)
    large = max_exact + (np.log(nf / max_exact) / math.log(REL_MAX_DIST / max_exact)
                         * (REL_BUCKETS - max_exact)).astype(np.int32)
    large = np.minimum(large, REL_BUCKETS - 1)
    return np.where(n < max_exact, n, large).astype(np.int32)


def _dsa_kernel(rb_ref, bk_ref, q_ref, k_ref, v_ref, qi_ref, kwa_ref, kwq_ref, o_ref,
                kb_ref, vt_ref, kia_ref, kib_ref, qs_ref, qst_ref, key_ref, hi_ref, lo_ref,
                negm_ref, bias_ref, s_ref, m_ref, acc_ref, pos_ref, *, seq, tq, topk):
    bi = pl.program_id(0)
    i = pl.program_id(1)
    t0 = i * tq
    sc = 2 * LANES
    nsc = seq // sc
    gt = GROUP * tq
    nskip = topk // tq
    log2e = math.log2(math.e)
    att_scale = ATT_DIM ** -0.5 * log2e
    idx_scale = IDX_HEADS ** -0.5 * IDX_DIM ** -0.5
    nsel = (t0 + tq + sc - 1) // sc

    @pl.when(i == 0)
    def _():
        lane = lax.broadcasted_iota(I32, (sc, LANES), 1)
        for u in range(nsc):
            kblk = k_ref[0, u * sc:(u + 1) * sc, :]
            vblk = v_ref[0, u * sc:(u + 1) * sc, :]
            for kh in range(KV_HEADS):
                kb_ref[kh, u] = kblk[:, kh * ATT_DIM:(kh + 1) * ATT_DIM].astype(BF16)
                vt_ref[kh, u] = vblk[:, kh * ATT_DIM:(kh + 1) * ATT_DIM].astype(F32).T.astype(BF16)
            kw = kwa_ref[0, u * sc:(u + 1) * sc, :].astype(F32)
            ka = jnp.where(lane < IDX_DIM, kw, 0.0)
            kia_ref[u] = ka.astype(BF16)
            kib_ref[u] = pltpu.roll(ka, IDX_DIM, 1).astype(BF16)

    @pl.when((bi == 0) & (i == 0))
    def _():
        bias_ref[...] = jnp.zeros(bias_ref.shape, F32)

        def fill(bu, c):
            for r in range(2):
                hit = bk_ref[r] == bu
                for h in range(ATT_HEADS):
                    kh, g = divmod(h, GROUP)
                    val = (rb_ref[bu, h] - rb_ref[REL_BUCKETS - 1, h]) * log2e
                    rows = slice((2 + r) * LANES, (3 + r) * LANES)
                    cur = bias_ref[kh, rows, g * tq:(g + 1) * tq]
                    bias_ref[kh, rows, g * tq:(g + 1) * tq] = jnp.where(hit, val, cur)
            return c

        lax.fori_loop(0, REL_BUCKETS, fill, 0)

    for h in range(ATT_HEADS):
        kh, g = divmod(h, GROUP)
        qh = q_ref[0, :, h * ATT_DIM:(h + 1) * ATT_DIM].astype(BF16)
        qs_ref[kh, :, g * tq:(g + 1) * tq] = qh.T

    row_s = lax.broadcasted_iota(I32, (sc, tq), 0)
    tidx = t0 + lax.broadcasted_iota(I32, (sc, tq), 1)

    @pl.when(i < nskip)
    def _():
        for u in range((nskip * tq + sc - 1) // sc):
            negm_ref[u] = jnp.where(u * sc + row_s <= tidx, 0.0, NEG_INF)

    @pl.when(i >= nskip)
    def _():
        for pp in range(IDX_HEADS // 4):
            for r in range(2):
                cols = slice((2 * pp + r) * LANES, (2 * pp + r + 1) * LANES)
                qst_ref[pp, :, r * tq:(r + 1) * tq] = qi_ref[0, :, cols].astype(BF16).T
        wt = kwq_ref[0].T * idx_scale

        def head_w(h):
            return wt[IDX_DIM + h:IDX_DIM + h + 1, :]

        def score_units(u0, nu):
            ka = jnp.concatenate([kia_ref[u0 + w] for w in range(nu)], axis=0)
            kb_ = jnp.concatenate([kib_ref[u0 + w] for w in range(nu)], axis=0)
            acc = jnp.zeros((nu * sc, tq), F32)
            for pp in range(IDX_HEADS // 4):
                q2 = qst_ref[pp]
                sa = jnp.maximum(_dot(ka, q2), 0.0)
                sb = jnp.maximum(_dot(kb_, q2), 0.0)
                acc = acc + sa[:, :tq] * head_w(4 * pp) + sb[:, :tq] * head_w(4 * pp + 1)
                acc = acc + sa[:, tq:] * head_w(4 * pp + 2) + sb[:, tq:] * head_w(4 * pp + 3)
            for w in range(nu):
                u = u0 + w
                score = jnp.where(u * sc + row_s <= tidx, acc[w * sc:(w + 1) * sc] + 0.0, NEG_INF)
                bits = lax.bitcast_convert_type(score, I32)
                key = jnp.where(bits < 0, bits ^ jnp.int32(0x7FFFFFFF), bits)
                key_ref[u] = key
                hi_ref[u] = lax.shift_right_arithmetic(key, 16).astype(I16)
                lo_ref[u] = ((key & 0xFFFF) - HALF16).astype(I16)

        def score_quad(qi, carry):
            score_units(4 * qi, 4)
            return carry

        lax.fori_loop(0, nsel // 4, score_quad, 0)

        @pl.when(nsel % 4 >= 2)
        def _():
            score_units((nsel // 4) * 4, 2)

        @pl.when(nsel % 2 == 1)
        def _():
            score_units(nsel - 1, 1)

        @pl.when(nsel % 2 == 1)
        def _():
            hi_ref[nsel] = jnp.full((sc, tq), -HALF16, I16)
            lo_ref[nsel] = jnp.full((sc, tq), -HALF16, I16)

        npair = (nsel + 1) // 2
        rows16 = 2 * SUBLANES
        one16, zero16 = jnp.int16(1), jnp.int16(0)

        def count16(ref, cand):
            c16 = jnp.broadcast_to(cand.astype(I16), (rows16, tq))

            def body(pi, accs):
                accs = list(accs)
                for w in range(2):
                    kk = ref[2 * pi + w]
                    for r in range(sc // rows16):
                        hit = jnp.where(kk[r * rows16:(r + 1) * rows16] >= c16, one16, zero16)
                        accs[r % len(accs)] = accs[r % len(accs)] + hit
                return tuple(accs)

            accs = lax.fori_loop(0, npair, body,
                                 tuple(jnp.zeros((rows16, tq), I16) for _ in range(4)))
            tot = (accs[0] + accs[1]) + (accs[2] + accs[3])
            return tot.astype(I32).sum(axis=0, keepdims=True)

        def search16(ref):
            def step(it, thr):
                cand = thr + lax.shift_left(jnp.int32(1), 15 - it)
                return jnp.where(count16(ref, cand) >= topk, cand, thr)
            return lax.fori_loop(0, 16, step, jnp.full((1, tq), -HALF16, I32))

        t_hi = search16(hi_ref)
        t_hi16 = jnp.broadcast_to(t_hi.astype(I16), (rows16, tq))

        def fold(pi, carry):
            for w in range(2):
                u = 2 * pi + w
                for r in range(sc // rows16):
                    rs = slice(r * rows16, (r + 1) * rows16)
                    hi = hi_ref[u, rs, :]
                    lo_ref[u, rs, :] = jnp.where(
                        hi > t_hi16, jnp.int16(HALF16 - 1),
                        jnp.where(hi == t_hi16, lo_ref[u, rs, :], jnp.int16(-HALF16)))
            return carry

        lax.fori_loop(0, npair, fold, 0)
        t_lo = search16(lo_ref)
        thr = t_hi * (2 * HALF16) + (t_lo + HALF16)

        def count(pred):
            def body(c2, acc8):
                hit = pred(key_ref[c2], c2 * sc + row_s).astype(I32)
                return acc8 + hit.reshape(sc // SUBLANES, SUBLANES, tq).sum(axis=0)
            acc8 = lax.fori_loop(0, nsel, body, jnp.zeros((SUBLANES, tq), I32))
            return acc8.sum(axis=0, keepdims=True)

        need = topk - count(lambda kk, sidx: kk > thr)
        n_tie = count(lambda kk, sidx: kk == thr)

        nbits = int(seq).bit_length()
        pos_ref[...] = jnp.full((1, tq), 2 ** nbits, I32)

        @pl.when(jnp.max(jnp.where(n_tie != need, 1, 0)) > 0)
        def _():
            def tie_step(it, pos):
                cand = pos + lax.shift_left(jnp.int32(1), nbits - 1 - it)
                cnt = count(lambda kk, sidx: (kk == thr) & (sidx < cand))
                return jnp.where(cnt <= need, cand, pos)

            pos_ref[...] = lax.fori_loop(0, nbits, tie_step, jnp.zeros((1, tq), I32))

        pos = pos_ref[...]

        def mask_unit(u, carry):
            kk = key_ref[u]
            sidx = u * sc + row_s
            sel = (kk > thr) | ((kk == thr) & (sidx < pos))
            negm_ref[u] = jnp.where(sel & (sidx <= tidx), 0.0, NEG_INF)
            return carry

        lax.fori_loop(0, nsel, mask_unit, 0)

    nfar = jnp.maximum((i - 1) // 2, 0)
    part = (sc // SUBLANES, SUBLANES, gt)

    def logits_unit(u, mparts, with_bias):
        nm = negm_ref[u]
        nm = jnp.concatenate([nm] * GROUP, axis=1)
        out = []
        for kh in range(KV_HEADS):
            s = _dot(kb_ref[kh, u], qs_ref[kh]) * att_scale + nm
            if with_bias:
                start = pl.multiple_of((2 * u - i + 3) * LANES, LANES)
                s = s + bias_ref[kh, pl.ds(start, sc), :]
            s_ref[kh, u] = s
            out.append(jnp.maximum(mparts[kh], s.reshape(part).max(axis=0)))
        return tuple(out)

    def far_units(u0, nu, mparts):
        nm = jnp.concatenate([negm_ref[u0 + w] for w in range(nu)], axis=0)
        nm = jnp.concatenate([nm] * GROUP, axis=1)
        out = []
        for kh in range(KV_HEADS):
            kk = jnp.concatenate([kb_ref[kh, u0 + w] for w in range(nu)], axis=0)
            s = _dot(kk, qs_ref[kh]) * att_scale + nm
            for w in range(nu):
                s_ref[kh, u0 + w] = s[w * sc:(w + 1) * sc]
            out.append(jnp.maximum(mparts[kh], s.reshape((nu * part[0],) + part[1:]).max(axis=0)))
        return tuple(out)

    mparts = tuple(jnp.full((SUBLANES, gt), -jnp.inf, F32) for _ in range(KV_HEADS))
    mparts = lax.fori_loop(0, nfar // 4, lambda qi, c: far_units(4 * qi, 4, c), mparts)
    mparts = lax.cond(nfar % 4 >= 2, lambda c: far_units((nfar // 4) * 4, 2, c), lambda c: c, mparts)
    mparts = lax.cond(nfar % 2 == 1, lambda c: far_units(nfar - 1, 1, c), lambda c: c, mparts)
    mparts = lax.fori_loop(nfar, nsel, lambda u, c: logits_unit(u, c, True), mparts)
    for kh in range(KV_HEADS):
        m_ref[kh] = jnp.broadcast_to(mparts[kh].max(axis=0, keepdims=True), (SUBLANES, gt))
    acc_ref[...] = jnp.zeros(acc_ref.shape, F32)

    def probs_units(u0, nu, lparts):
        out = []
        for kh in range(KV_HEADS):
            lp = lparts[kh]
            ps = []
            for w in range(nu):
                p = jnp.exp2(s_ref[kh, u0 + w].reshape(part) - m_ref[kh][None])
                lp = lp + p.sum(axis=0)
                ps.append(p.reshape(sc, gt).astype(BF16))
            vt = jnp.concatenate([vt_ref[kh, u0 + w] for w in range(nu)], axis=1)
            acc_ref[kh] += _dot(vt, jnp.concatenate(ps, axis=0))
            out.append(lp)
        return tuple(out)

    lparts = tuple(jnp.zeros((SUBLANES, gt), F32) for _ in range(KV_HEADS))
    lparts = lax.fori_loop(0, nsel // 4, lambda qi, c: probs_units(4 * qi, 4, c), lparts)
    lparts = lax.cond(nsel % 4 >= 2, lambda c: probs_units((nsel // 4) * 4, 2, c), lambda c: c, lparts)
    lparts = lax.cond(nsel % 2 == 1, lambda c: probs_units(nsel - 1, 1, c), lambda c: c, lparts)
    for kh in range(KV_HEADS):
        out_t = acc_ref[kh] / lparts[kh].sum(axis=0, keepdims=True)
        for g in range(GROUP):
            h = kh * GROUP + g
            o_ref[0, :, h * ATT_DIM:(h + 1) * ATT_DIM] = (
                out_t[:, g * tq:(g + 1) * tq].astype(o_ref.dtype).T)


def dsa_attention(proj, kw_f32, rel_bias, *, seq, tq=LANES):
    assert tq == LANES
    b = proj.shape[0]
    topk = min(TOPK_MAX, seq // 4)
    assert topk % tq == 0 and seq % (2 * LANES) == 0
    o_q = ATT_HEADS * ATT_DIM
    o_kv = KV_HEADS * ATT_DIM
    o_qi = IDX_HEADS * IDX_DIM
    sc = 2 * LANES
    nsc = seq // sc
    gt = GROUP * tq
    jj, ii = np.meshgrid(np.arange(tq), np.arange(tq), indexing="ij")
    buckets = jnp.asarray(np.stack([_t5_bucket_np(tq + ii - jj), _t5_bucket_np(ii - jj)]))
    once = pl.Buffered(1)
    in_specs = [
        pl.BlockSpec(memory_space=pltpu.SMEM),
        pl.BlockSpec((2, tq, tq), lambda bi, i: (0, 0, 0)),
        pl.BlockSpec((1, tq, o_q), lambda bi, i: (bi, i, 0)),
        pl.BlockSpec((1, seq, o_kv), lambda bi, i: (bi, 0, o_q // o_kv), pipeline_mode=once),
        pl.BlockSpec((1, seq, o_kv), lambda bi, i: (bi, 0, o_q // o_kv + 1), pipeline_mode=once),
        pl.BlockSpec((1, tq, o_qi), lambda bi, i: (bi, i, (o_q + 2 * o_kv) // o_qi)),
        pl.BlockSpec((1, seq, LANES), lambda bi, i: (bi, 0, (o_q + 2 * o_kv + o_qi) // LANES),
                     pipeline_mode=once),
        pl.BlockSpec((1, tq, LANES), lambda bi, i: (bi, i, 0)),
    ]
    scratch = [
        pltpu.VMEM((KV_HEADS, nsc, sc, ATT_DIM), BF16),
        pltpu.VMEM((KV_HEADS, nsc, ATT_DIM, sc), BF16),
        pltpu.VMEM((nsc, sc, LANES), BF16),
        pltpu.VMEM((nsc, sc, LANES), BF16),
        pltpu.VMEM((KV_HEADS, ATT_DIM, gt), BF16),
        pltpu.VMEM((IDX_HEADS // 4, LANES, 2 * tq), BF16),
        pltpu.VMEM((nsc, sc, tq), I32),
        pltpu.VMEM((nsc, sc, tq), I16),
        pltpu.VMEM((nsc, sc, tq), I16),
        pltpu.VMEM((nsc, sc, tq), F32),
        pltpu.VMEM((KV_HEADS, 5 * LANES, gt), F32),
        pltpu.VMEM((KV_HEADS, nsc, sc, gt), F32),
        pltpu.VMEM((KV_HEADS, SUBLANES, gt), F32),
        pltpu.VMEM((KV_HEADS, ATT_DIM, gt), F32),
        pltpu.VMEM((1, tq), I32),
    ]
    return pl.pallas_call(
        functools.partial(_dsa_kernel, seq=seq, tq=tq, topk=topk),
        grid=(b, seq // tq),
        in_specs=in_specs,
        out_specs=pl.BlockSpec((1, tq, o_q), lambda bi, i: (bi, i, 0)),
        out_shape=jax.ShapeDtypeStruct((b, seq, o_q), BF16),
        scratch_shapes=scratch,
        compiler_params=_cparams(2, 56),
        name="dsa_attention",
    )(rel_bias, buckets, proj, proj, proj, proj, proj, kw_f32)


def kernel(x, norm_mix, norm_ffn, e_w_in, e_ret_gn, e_conv_w, e_conv_b, e_gate_a_w, e_gate_a_b,
           e_gate_x_w, e_gate_x_b, e_lambda, e_w_out, o_w_in, o_w_out, rel_bias,
           ffn_w_gu, ffn_conv_w, ffn_conv_b, ffn_w_down, final_norm):
    b, s, d = x.shape
    m = b * s
    x2 = x.reshape(m, d)

    e_in = e_w_in.shape[-1]
    proj = norm_matmul(x2, norm_mix[0], cast_layer(e_w_in, 0), tm=PROJ_TM, tn=PROJ_TN)
    proj = proj.reshape(b, s, e_in)
    ret = retention(proj, e_ret_gn[0], seq=s)
    lru = rglru(proj, e_conv_w[0], e_conv_b[0], e_gate_a_w[0], e_gate_a_b[0], e_gate_x_w[0],
                e_gate_x_b[0], e_lambda[0], seq=s,
                x_off=4 * RET_HEADS, y_off=4 * RET_HEADS + LRU_BLOCKS)
    x2 = matmul_residual(x2, ret.reshape(m, RET_WIDTH), 0, lru.reshape(m, LRU_WIDTH), 0,
                         cast_layer(e_w_out, 0), tm=OUT_TM)
    x2 = conv_ffn(x2, norm_ffn[0], cast_layer(ffn_w_gu, 0), ffn_conv_w[0], ffn_conv_b[0],
                  cast_layer(ffn_w_down, 0), final_norm, seq=s, tm=FFN_TM, tf=FFN_TF,
                  final_norm=False)

    o_in = o_w_in.shape[-1]
    n_pad = -o_in % PROJ_TN
    w_in = jnp.pad(o_w_in[0].astype(BF16), ((0, 0), (0, n_pad)))
    kw_col = ATT_HEADS * ATT_DIM + 2 * KV_HEADS * ATT_DIM + IDX_HEADS * IDX_DIM
    proj, kw_f32 = norm_matmul(x2, norm_mix[1], w_in, tm=PROJ_TM, tn=PROJ_TN, out_dtype=BF16,
                               side_col=kw_col)
    attn = dsa_attention(proj.reshape(b, s, o_in + n_pad), kw_f32.reshape(b, s, LANES),
                         rel_bias, seq=s)
    attn = attn.reshape(m, ATT_HEADS * ATT_DIM)
    x2 = matmul_residual(x2, attn, 0, attn, 1, cast_layer(o_w_out, 0), tm=OUT_TM)
    x2 = conv_ffn(x2, norm_ffn[1], cast_layer(ffn_w_gu, 1), ffn_conv_w[1], ffn_conv_b[1],
                  cast_layer(ffn_w_down, 1), final_norm, seq=s, tm=FFN_TM, tf=FFN_TF,
                  final_norm=True)
    return x2.reshape(b, s, d)
```
